```python
import math
import jax, jax.numpy as jnp
from jax import lax
import numpy as np

D_MODEL = 1024
BATCH = 4
SEQ = 8192
DEPTH = 4

CTX_LEN = 256
GRID_W = 64
HEAD_DIM = 64
ATTN_SCALE = HEAD_DIM ** -0.5
ROPE_THETA = 10000.0
NEG_INF = -1e30
Q_BLOCK = 128

NA_HEADS = 8
NA_KH = 8
NA_KW = 16
SWA_Q_HEADS = 8
SWA_KV_HEADS = 2
SWA_WINDOW = 128
GA_Q_HEADS = 8
GA_KV_HEADS = 2
SSM_HEADS = 8
SSM_HEAD_DIM = 64
SSM_INNER = SSM_HEADS * SSM_HEAD_DIM
SSM_GROUPS = 2
SSM_STATE = 128
SSM_CONV = 5
SSM_CONV_CH = SSM_INNER + 2 * SSM_GROUPS * SSM_STATE
SSM_CHUNK = 128
N_BRANCH = 4
BRANCH_W = 512
D_FF = 2816
N_EXPERTS = 8
TOP_K = 2
D_FF_EXPERT = 3584
MOE_BLOCK = 512
ALPHA = (2 * DEPTH) ** 0.25
BETA = (8 * DEPTH) ** -0.25
LN_EPS = 1e-5
RMS_EPS = 1e-6

IN_SPLITS = (
    NA_HEADS * HEAD_DIM, NA_HEADS * HEAD_DIM, NA_HEADS * HEAD_DIM,
    SWA_Q_HEADS * HEAD_DIM, SWA_KV_HEADS * HEAD_DIM, SWA_KV_HEADS * HEAD_DIM,
    GA_Q_HEADS * HEAD_DIM, GA_KV_HEADS * HEAD_DIM, GA_KV_HEADS * HEAD_DIM,
    SSM_INNER, SSM_CONV_CH, 2 * SSM_HEADS,
)
D_IN = sum(IN_SPLITS)

kernel_name = 'hybrid_gated_na_swa_ga_ssd_moe_dit'


def layer_norm(x, g, b):
    xf = x.astype(jnp.float32)
    mu = jnp.mean(xf, -1, keepdims=True)
    var = jnp.mean(jnp.square(xf - mu), -1, keepdims=True)
    y = (xf - mu) * lax.rsqrt(var + LN_EPS) * g.astype(jnp.float32) + b.astype(jnp.float32)
    return y.astype(x.dtype)


def rms_norm(x, g):
    xf = x.astype(jnp.float32)
    y = xf * lax.rsqrt(jnp.mean(jnp.square(xf), -1, keepdims=True) + RMS_EPS) * g.astype(jnp.float32)
    return y.astype(x.dtype)


def heads(t):
    return t.reshape(t.shape[0], t.shape[1], -1, HEAD_DIM)


def group(t, n_kv):
    return t.reshape(t.shape[0], t.shape[1], n_kv, -1, HEAD_DIM)


def split_cols(p):
    return jnp.split(p, np.cumsum(IN_SPLITS)[:-1].tolist(), axis=-1)


def axial_rope_tables(n_tok):
    t = jnp.arange(n_tok, dtype=jnp.int32)
    row = (t // GRID_W).astype(jnp.float32)
    col = (t % GRID_W).astype(jnp.float32)
    n_freq = HEAD_DIM // 4
    inv_freq = ROPE_THETA ** (-jnp.arange(n_freq, dtype=jnp.float32) / n_freq)
    ang = jnp.concatenate([row[:, None] * inv_freq, col[:, None] * inv_freq], axis=-1)
    return jnp.cos(ang), jnp.sin(ang)


def apply_rope(x, cos, sin):
    half = HEAD_DIM // 2
    x1, x2 = x[..., :half], x[..., half:]
    cs = cos[:, None, :].astype(x.dtype)
    sn = sin[:, None, :].astype(x.dtype)
    return jnp.concatenate([x1 * cs - x2 * sn, x2 * cs + x1 * sn], axis=-1)


def attend(q, k, v, mask=None, sink=None):
    s = jnp.einsum('bqkgd,bjkd->bkgqj', q, k).astype(jnp.float32) * ATTN_SCALE
    if mask is not None:
        s = jnp.where(mask, s, NEG_INF)
    if sink is not None:
        s_sink = jnp.broadcast_to(sink.astype(jnp.float32)[None, :, :, None, None], s.shape[:-1] + (1,))
        s = jnp.concatenate([s, s_sink], axis=-1)
    p = jax.nn.softmax(s, axis=-1)
    if sink is not None:
        p = p[..., :-1]
    return jnp.einsum('bkgqj,bjkd->bqkgd', p.astype(v.dtype), v)


def neighbourhood_attention(q, k, v, qc, kc, vc, rpb, with_ctx):
    bsz, n_tok, n_h, dh = q.shape
    rows = n_tok // GRID_W
    kh = min(NA_KH, rows)
    qg = q.reshape(bsz, rows, GRID_W, n_h, dh)
    kg = k.reshape(bsz, rows, GRID_W, n_h, dh)
    vg = v.reshape(bsz, rows, GRID_W, n_h, dh)
    r_idx = jnp.arange(rows, dtype=jnp.int32)
    row_start = jnp.clip(r_idx - kh // 2, 0, rows - kh)
    c_idx = jnp.arange(GRID_W, dtype=jnp.int32)
    col_start = jnp.clip(c_idx - NA_KW // 2, 0, GRID_W - NA_KW)
    col_idx = col_start[:, None] + jnp.arange(NA_KW, dtype=jnp.int32)
    dc = col_idx - c_idx[:, None] + NA_KW - 1
    n_loc = kh * NA_KW

    def row_fn(args):
        r, start, q_row = args
        kb = lax.dynamic_slice_in_dim(kg, start, kh, axis=1)[:, :, col_idx]
        vb = lax.dynamic_slice_in_dim(vg, start, kh, axis=1)[:, :, col_idx]
        dr = start + jnp.arange(kh, dtype=jnp.int32) - r + NA_KH - 1
        bias = rpb[:, dr][:, :, dc].transpose(0, 2, 1, 3).astype(jnp.float32)
        s_loc = jnp.einsum('bqhd,bkqwhd->bhqkw', q_row, kb).astype(jnp.float32) * ATTN_SCALE + bias[None]
        s_ctx = jnp.einsum('bqhd,bjhd->bhqj', q_row, kc).astype(jnp.float32) * ATTN_SCALE
        s = jnp.concatenate([s_loc.reshape(bsz, n_h, GRID_W, n_loc), s_ctx], axis=-1)
        p = jax.nn.softmax(s, axis=-1).astype(v.dtype)
        p_loc = p[..., :n_loc].reshape(bsz, n_h, GRID_W, kh, NA_KW)
        return (jnp.einsum('bhqkw,bkqwhd->bqhd', p_loc, vb)
                + jnp.einsum('bhqj,bjhd->bqhd', p[..., n_loc:], vc))

    out = lax.map(row_fn, (r_idx, row_start, qg.transpose(1, 0, 2, 3, 4)))
    out = out.transpose(1, 0, 2, 3, 4).reshape(bsz, n_tok, n_h * dh)
    out_c = attend(qc[:, :, :, None], kc, vc).reshape(bsz, qc.shape[1], -1) if with_ctx else None
    return out, out_c


def window_attention(q, k, v, qc, kc, vc, sink, with_ctx):
    bsz, n_tok = q.shape[:2]
    n_blk = n_tok // Q_BLOCK
    band = Q_BLOCK + 2 * SWA_WINDOW
    kp = jnp.pad(k, ((0, 0), (SWA_WINDOW, SWA_WINDOW), (0, 0), (0, 0)))
    vp = jnp.pad(v, ((0, 0), (SWA_WINDOW, SWA_WINDOW), (0, 0), (0, 0)))
    qi = jnp.arange(Q_BLOCK, dtype=jnp.int32)
    kj = jnp.arange(band, dtype=jnp.int32)
    in_win = jnp.abs(kj[None, :] - SWA_WINDOW - qi[:, None]) <= SWA_WINDOW
    ctx_ok = jnp.ones((Q_BLOCK, kc.shape[1]), dtype=bool)

    def block(args):
        n, qb = args
        kb = lax.dynamic_slice_in_dim(kp, n * Q_BLOCK, band, axis=1)
        vb = lax.dynamic_slice_in_dim(vp, n * Q_BLOCK, band, axis=1)
        key_pos = n * Q_BLOCK - SWA_WINDOW + kj
        valid = in_win & ((key_pos >= 0) & (key_pos < n_tok))[None, :]
        mask = jnp.concatenate([valid, ctx_ok], axis=-1)
        return attend(qb, jnp.concatenate([kb, kc], axis=1), jnp.concatenate([vb, vc], axis=1), mask, sink)

    qbl = q.reshape(bsz, n_blk, Q_BLOCK, *q.shape[2:]).swapaxes(0, 1)
    out = lax.map(block, (jnp.arange(n_blk, dtype=jnp.int32), qbl)).swapaxes(0, 1).reshape(bsz, n_tok, -1)
    out_c = attend(qc, kc, vc, sink=sink).reshape(bsz, qc.shape[1], -1) if with_ctx else None
    return out, out_c


def global_attention(q, k, v, qc, kc, vc, with_ctx):
    bsz, n_tok = q.shape[:2]
    n_blk = n_tok // Q_BLOCK
    k_all = jnp.concatenate([k, kc], axis=1)
    v_all = jnp.concatenate([v, vc], axis=1)
    qbl = q.reshape(bsz, n_blk, Q_BLOCK, *q.shape[2:]).swapaxes(0, 1)
    out = lax.map(lambda qb: attend(qb, k_all, v_all), qbl).swapaxes(0, 1).reshape(bsz, n_tok, -1)
    out_c = attend(qc, kc, vc).reshape(bsz, qc.shape[1], -1) if with_ctx else None
    return out, out_c


def dwconv_silu(u, w, b):
    ch = u.shape[-1]
    y = lax.conv_general_dilated(u, w[:, None, :].astype(u.dtype), window_strides=(1,),
                                 padding=[(SSM_CONV // 2, SSM_CONV // 2)],
                                 dimension_numbers=('NWC', 'WIO', 'NWC'), feature_group_count=ch)
    return jax.nn.silu(y + b)


def segsum_exp(a):
    q_len = a.shape[-1]
    cs = jnp.cumsum(a, axis=-1)
    diff = cs[..., :, None] - cs[..., None, :]
    mask = jnp.tril(jnp.ones((q_len, q_len), dtype=bool))
    return jnp.where(mask, jnp.exp(jnp.where(mask, diff, 0.0)), 0.0)


def ssd_scan(x, dt, a, bm, cm, h0, return_y):
    bsz, n_tok, n_h, p_dim = x.shape
    n_st = bm.shape[-1]
    n_c = n_tok // SSM_CHUNK
    xdt = (x.astype(jnp.float32) * dt[..., None]).reshape(bsz, n_c, SSM_CHUNK, n_h, p_dim)
    da = (dt * a.astype(jnp.float32)).reshape(bsz, n_c, SSM_CHUNK, n_h).transpose(0, 3, 1, 2)
    a_cum = jnp.cumsum(da, axis=-1)
    bc = bm.astype(jnp.float32).reshape(bsz, n_c, SSM_CHUNK, n_h, n_st)
    cc = cm.astype(jnp.float32).reshape(bsz, n_c, SSM_CHUNK, n_h, n_st)
    decay_states = jnp.exp(a_cum[..., -1:] - a_cum)
    states = jnp.einsum('bclhn,bhcl,bclhp->bchpn', bc, decay_states, xdt)
    chunk_decay = jnp.exp(a_cum[..., -1])

    def step(h, inp):
        st, dec = inp
        return h * dec[..., None, None] + st, (h if return_y else None)

    h_final, h_in = lax.scan(step, h0, (states.transpose(1, 0, 2, 3, 4), chunk_decay.transpose(2, 0, 1)))
    if not return_y:
        return None, h_final
    h_in = h_in.transpose(1, 0, 2, 3, 4)
    y_diag = jnp.einsum('bclhn,bcshn,bhcls,bcshp->bclhp', cc, bc, segsum_exp(da), xdt)
    y_off = jnp.einsum('bclhn,bchpn,bhcl->bclhp', cc, h_in, jnp.exp(a_cum))
    return (y_diag + y_off).reshape(bsz, n_tok, n_h, p_dim), h_final


def mamba_branch(z, xbc, dt_raw, zc, xbcc, dtc_raw, conv_w, conv_b, dt_bias, a_log, d_skip, norm_g, with_ctx):
    a = -jnp.exp(a_log.astype(jnp.float32))
    rep = SSM_HEADS // SSM_GROUPS
    gn = SSM_GROUPS * SSM_STATE

    def prep(xbc_, dt_raw_):
        u = dwconv_silu(xbc_, conv_w, conv_b)
        bsz, n_tok = u.shape[:2]
        xs = u[..., :SSM_INNER].reshape(bsz, n_tok, SSM_HEADS, SSM_HEAD_DIM)
        bm = jnp.repeat(u[..., SSM_INNER:SSM_INNER + gn].reshape(bsz, n_tok, SSM_GROUPS, SSM_STATE), rep, axis=2)
        cm = jnp.repeat(u[..., SSM_INNER + gn:].reshape(bsz, n_tok, SSM_GROUPS, SSM_STATE), rep, axis=2)
        dt = jax.nn.softplus(dt_raw_.astype(jnp.float32).reshape(bsz, n_tok, 2, SSM_HEADS) + dt_bias.astype(jnp.float32))
        return xs, bm, cm, dt

    xl, bl, cl, dtl = prep(xbc, dt_raw)
    xc, bc, cc, dtc = prep(xbcc, dtc_raw)
    fl = lambda t: jnp.flip(t, axis=1)
    h0 = jnp.zeros((xl.shape[0], SSM_HEADS, SSM_HEAD_DIM, SSM_STATE), jnp.float32)
    yc_f, hc_f = ssd_scan(xc, dtc[:, :, 0], a[0], bc, cc, h0, with_ctx)
    yc_b, hc_b = ssd_scan(fl(xc), fl(dtc[:, :, 1]), a[1], fl(bc), fl(cc), h0, with_ctx)
    yl_f, _ = ssd_scan(xl, dtl[:, :, 0], a[0], bl, cl, hc_f, True)
    yl_b, _ = ssd_scan(fl(xl), fl(dtl[:, :, 1]), a[1], fl(bl), fl(cl), hc_b, True)

    def finish(y, xs, z_):
        y = y + d_skip.astype(jnp.float32)[:, None] * xs.astype(jnp.float32)
        y = y.reshape(y.shape[0], y.shape[1], SSM_INNER)
        return rms_norm(y * jax.nn.silu(z_.astype(jnp.float32)), norm_g).astype(z_.dtype)

    out = finish(yl_f + fl(yl_b), xl, z)
    out_c = finish(yc_f + fl(yc_b), xc, zc) if with_ctx else None
    return out, out_c


def hybrid_mixer(h, hc, w_in, w_gate, b_gate, na_rpb, swa_sink, qk_gain_q, qk_gain_k, conv_w, conv_b,
                 dt_bias, a_log, d_skip, ssm_norm_g, w_branch, w_out, with_ctx):
    cos, sin = axial_rope_tables(h.shape[1])
    rope = lambda t: apply_rope(t, cos, sin)
    (na_q, na_k, na_v, sw_q, sw_k, sw_v, ga_q, ga_k, ga_v, z, xbc, dt_raw) = split_cols(h @ w_in)
    (na_qc, na_kc, na_vc, sw_qc, sw_kc, sw_vc, ga_qc, ga_kc, ga_vc, zc, xbcc, dtc_raw) = split_cols(hc @ w_in)
    o_a, o_ac = neighbourhood_attention(heads(na_q), heads(na_k), heads(na_v),
                                        heads(na_qc), heads(na_kc), heads(na_vc), na_rpb, with_ctx)
    o_b, o_bc = window_attention(group(rope(heads(sw_q)), SWA_KV_HEADS), rope(heads(sw_k)), heads(sw_v),
                                 group(heads(sw_qc), SWA_KV_HEADS), heads(sw_kc), heads(sw_vc),
                                 swa_sink.reshape(SWA_KV_HEADS, -1), with_ctx)
    o_c, o_cc = global_attention(group(rope(rms_norm(heads(ga_q), qk_gain_q)), GA_KV_HEADS),
                                 rope(rms_norm(heads(ga_k), qk_gain_k)), heads(ga_v),
                                 group(rms_norm(heads(ga_qc), qk_gain_q), GA_KV_HEADS),
                                 rms_norm(heads(ga_kc), qk_gain_k), heads(ga_vc), with_ctx)
    o_d, o_dc = mamba_branch(z, xbc, dt_raw, zc, xbcc, dtc_raw, conv_w, conv_b, dt_bias, a_log, d_skip,
                             ssm_norm_g, with_ctx)

    def merge(h_in, branches):
        terms = [jax.nn.sigmoid(h_in @ w_gate[i] + b_gate[i]) * (u @ w_branch[i]) for i, u in enumerate(branches)]
        return (terms[0] + terms[1] + terms[2] + terms[3]) @ w_out

    out = merge(h, (o_a, o_b, o_c, o_d))
    out_c = merge(hc, (o_ac, o_bc, o_cc, o_dc)) if with_ctx else None
    return out, out_c


def swiglu(h, w_up, w_down):
    g, u = jnp.split(h @ w_up, 2, axis=-1)
    return (jax.nn.silu(g) * u) @ w_down


def moe_swiglu(tokens, w_router, b_router, w_up, w_down):
    n_tok, d = tokens.shape
    logits = (tokens @ w_router).astype(jnp.float32) + b_router.astype(jnp.float32)
    top_logit, top_idx = lax.top_k(logits, TOP_K)
    gate = jax.nn.softmax(top_logit, axis=-1)
    n_slot = n_tok * TOP_K
    expert = top_idx.reshape(-1)
    token = jnp.repeat(jnp.arange(n_tok, dtype=jnp.int32), TOP_K)
    order = jnp.argsort(expert)
    e_s, t_s, w_s = expert[order], token[order], gate.reshape(-1)[order]
    counts = jax.ops.segment_sum(jnp.ones_like(expert), expert, num_segments=N_EXPERTS)
    padded = (counts + MOE_BLOCK - 1) // MOE_BLOCK * MOE_BLOCK
    grp_start = jnp.cumsum(counts) - counts
    pad_end = jnp.cumsum(padded)
    pad_start = pad_end - padded
    dest = pad_start[e_s] + jnp.arange(n_slot, dtype=jnp.int32) - grp_start[e_s]
    n_blk = -(-n_slot // MOE_BLOCK) + N_EXPERTS
    x_pad = jnp.zeros((n_blk * MOE_BLOCK, d), tokens.dtype).at[dest].set(tokens[t_s])
    blk_expert = jnp.minimum(
        jnp.searchsorted(pad_end, jnp.arange(n_blk, dtype=jnp.int32) * MOE_BLOCK, side='right'), N_EXPERTS - 1)

    def expert_block(args):
        xb, e = args
        return swiglu(xb, w_up[e], w_down[e])

    y_pad = lax.map(expert_block, (x_pad.reshape(n_blk, MOE_BLOCK, d), blk_expert)).reshape(-1, d)
    y = y_pad[dest] * w_s[:, None].astype(tokens.dtype)
    return jnp.zeros_like(tokens).at[t_s].add(y)


def setup_inputs(seed: int = 0) -> dict:
    key = jax.random.key(seed)
    ks = iter(jax.random.split(key, 48))
    nrm = lambda shape, scale: jax.random.normal(next(ks), shape, jnp.float32) * scale
    d = D_MODEL
    n_l = DEPTH
    n_dense = (DEPTH + 1) // 2
    n_moe = DEPTH // 2
    x = nrm((BATCH, SEQ, d), 1.0)
    c = nrm((BATCH, d), 1.0)
    ctx = nrm((BATCH, CTX_LEN, d), 1.0)
    c_ctx = nrm((d,), 1.0)
    w_mod = nrm((n_l, d, 6 * d), d ** -0.5)
    b_mod = nrm((n_l, 6 * d), 0.02)
    w_in = nrm((n_l, d, D_IN), d ** -0.5)
    w_gate = nrm((n_l, N_BRANCH, d, d), d ** -0.5)
    b_gate = nrm((n_l, N_BRANCH, d), 0.02)
    na_rpb = nrm((n_l, NA_HEADS, 2 * NA_KH - 1, 2 * NA_KW - 1), 0.5)
    swa_sink = nrm((n_l, SWA_Q_HEADS), 1.0)
    qk_gain_q = 1.0 + nrm((n_l, HEAD_DIM), 0.02)
    qk_gain_k = 1.0 + nrm((n_l, HEAD_DIM), 0.02)
    conv_w = nrm((n_l, SSM_CONV, SSM_CONV_CH), SSM_CONV ** -0.5)
    conv_b = nrm((n_l, SSM_CONV_CH), 0.02)
    dt0 = jnp.exp(jax.random.uniform(next(ks), (n_l, 2, SSM_HEADS), jnp.float32,
                                     minval=math.log(1e-3), maxval=math.log(1e-1)))
    dt_bias = dt0 + jnp.log(-jnp.expm1(-dt0))
    a_log = jnp.log(jax.random.uniform(next(ks), (n_l, 2, SSM_HEADS), jnp.float32, minval=1.0, maxval=16.0))
    d_skip = 1.0 + nrm((n_l, SSM_HEADS), 0.1)
    ssm_norm_g = 1.0 + nrm((n_l, SSM_INNER), 0.02)
    w_branch = nrm((n_l, N_BRANCH, BRANCH_W, d), BRANCH_W ** -0.5)
    w_out = nrm((n_l, d, d), d ** -0.5 * BETA)
    ln1_g = 1.0 + nrm((n_l, d), 0.02)
    ln1_b = nrm((n_l, d), 0.02)
    ln2_g = 1.0 + nrm((n_l, d), 0.02)
    ln2_b = nrm((n_l, d), 0.02)
    ffn_w_up = nrm((n_dense, d, 2 * D_FF), d ** -0.5)
    ffn_w_down = nrm((n_dense, D_FF, d), D_FF ** -0.5 * BETA)
    moe_w_router = nrm((n_moe, d, N_EXPERTS), d ** -0.5)
    moe_b_router = nrm((n_moe, N_EXPERTS), 0.01)
    moe_w_up = nrm((n_moe, N_EXPERTS, d, 2 * D_FF_EXPERT), d ** -0.5)
    moe_w_down = nrm((n_moe, N_EXPERTS, D_FF_EXPERT, d), D_FF_EXPERT ** -0.5 * BETA)
    return {'x': x, 'c': c, 'ctx': ctx, 'c_ctx': c_ctx, 'w_mod': w_mod, 'b_mod': b_mod, 'w_in': w_in,
            'w_gate': w_gate, 'b_gate': b_gate, 'na_rpb': na_rpb, 'swa_sink': swa_sink,
            'qk_gain_q': qk_gain_q, 'qk_gain_k': qk_gain_k, 'conv_w': conv_w, 'conv_b': conv_b,
            'dt_bias': dt_bias, 'a_log': a_log, 'd_skip': d_skip, 'ssm_norm_g': ssm_norm_g,
            'w_branch': w_branch, 'w_out': w_out, 'ln1_g': ln1_g, 'ln1_b': ln1_b, 'ln2_g': ln2_g,
            'ln2_b': ln2_b, 'ffn_w_up': ffn_w_up, 'ffn_w_down': ffn_w_down, 'moe_w_router': moe_w_router,
            'moe_b_router': moe_b_router, 'moe_w_up': moe_w_up, 'moe_w_down': moe_w_down}


def reference(x, c, ctx, c_ctx, w_mod, b_mod, w_in, w_gate, b_gate, na_rpb, swa_sink, qk_gain_q, qk_gain_k,
              conv_w, conv_b, dt_bias, a_log, d_skip, ssm_norm_g, w_branch, w_out, ln1_g, ln1_b, ln2_g, ln2_b,
              ffn_w_up, ffn_w_down, moe_w_router, moe_b_router, moe_w_up, moe_w_down):
    s_lat = jax.nn.silu(c)
    s_ctx = jax.nn.silu(c_ctx)
    for l in range(DEPTH):
        with_ctx = l < DEPTH - 1
        m = jnp.split((s_lat @ w_mod[l] + b_mod[l])[:, None, :], 6, axis=-1)
        mc = jnp.split(s_ctx @ w_mod[l] + b_mod[l], 6, axis=-1)
        h = x * (1 + m[1]) + m[0]
        hc = ctx * (1 + mc[1]) + mc[0]
        o, oc = hybrid_mixer(h, hc, w_in[l], w_gate[l], b_gate[l], na_rpb[l], swa_sink[l], qk_gain_q[l],
                             qk_gain_k[l], conv_w[l], conv_b[l], dt_bias[l], a_log[l], d_skip[l],
                             ssm_norm_g[l], w_branch[l], w_out[l], with_ctx)
        x = layer_norm(ALPHA * x + m[2] * o, ln1_g[l], ln1_b[l])
        h2 = x * (1 + m[4]) + m[3]
        if with_ctx:
            ctx = layer_norm(ALPHA * ctx + mc[2] * oc, ln1_g[l], ln1_b[l])
            h2c = ctx * (1 + mc[4]) + mc[3]
        i = l // 2
        if l % 2 == 0:
            f = swiglu(h2, ffn_w_up[i], ffn_w_down[i])
            if with_ctx:
                fc = swiglu(h2c, ffn_w_up[i], ffn_w_down[i])
        elif with_ctx:
            n_lat = h2.shape[0] * h2.shape[1]
            tok = jnp.concatenate([h2.reshape(-1, D_MODEL), h2c.reshape(-1, D_MODEL)], axis=0)
            y = moe_swiglu(tok, moe_w_router[i], moe_b_router[i], moe_w_up[i], moe_w_down[i])
            f = y[:n_lat].reshape(h2.shape)
            fc = y[n_lat:].reshape(h2c.shape)
        else:
            f = moe_swiglu(h2.reshape(-1, D_MODEL), moe_w_router[i], moe_b_router[i], moe_w_up[i],
                           moe_w_down[i]).reshape(h2.shape)
        x = layer_norm(ALPHA * x + m[5] * f, ln2_g[l], ln2_b[l])
        if with_ctx:
            ctx = layer_norm(ALPHA * ctx + mc[5] * fc, ln2_g[l], ln2_b[l])
    return x
```

```python
import functools
import math

import jax
import jax.numpy as jnp
from jax import lax
from jax.experimental import pallas as pl
from jax.experimental.pallas import tpu as pltpu

F32 = jnp.float32
BF16 = jnp.bfloat16
HI = lax.Precision.HIGHEST

D_MODEL = 1024
GRID_W = 64
HEAD_DIM = 64
ATTN_SCALE = HEAD_DIM ** -0.5
ROPE_THETA = 10000.0
NEG_INF = -1e30
NA_HEADS = 8
NA_KH = 8
NA_KW = 16
SWA_WINDOW = 128
SSM_HEADS = 8
SSM_INNER = 512
SSM_STATE = 128
SSM_CONV = 5
N_BRANCH = 4
BRANCH_W = 512
D_FF = 2816
N_EXPERTS = 8
D_FF_EXPERT = 3584
MOE_BLOCK = 512
LN_EPS = 1e-5
RMS_EPS = 1e-6

P_COLS = 4608
C512_NAQ, C512_NAK, C512_NAV, C512_SWQ, C512_GAQ, C512_Z = 0, 1, 2, 3, 4, 5
C1024_XBC = 3
C128_SWK, C128_SWV, C128_GAK, C128_GAV = 32, 33, 34, 35
DT_COLS = 256

VMEM_LIMIT = 48 * 1024 * 1024


def _cp(*sem):
    return pltpu.CompilerParams(dimension_semantics=sem, vmem_limit_bytes=VMEM_LIMIT)


def _sigmoid(x):
    return 1.0 / (1.0 + jnp.exp(-x))


def _silu(x):
    return x * _sigmoid(x)


def _softplus(x):
    return jnp.maximum(x, 0.0) + jnp.log1p(jnp.exp(-jnp.abs(x)))


def _layer_norm(y, g, b):
    mu = jnp.mean(y, axis=-1, keepdims=True)
    yc = y - mu
    var = jnp.mean(yc * yc, axis=-1, keepdims=True)
    return yc * lax.rsqrt(var + LN_EPS) * g + b


def _dot(a, b):
    return jnp.dot(a, b, preferred_element_type=F32)


def _dot_nt(a, b):
    return lax.dot_general(a, b, (((1,), (1,)), ((), ())), preferred_element_type=F32)


def _mod_kernel(s_ref, w_ref, b_ref, o_ref):
    s = _silu(s_ref[...])
    o_ref[0] = jnp.dot(s, w_ref[0], precision=HI, preferred_element_type=F32) + b_ref[0]


def _modulation(s_rows, w_mod, b_mod):
    n_l, d, d6 = w_mod.shape
    tn = 1024
    return pl.pallas_call(
        _mod_kernel,
        grid=(n_l, d6 // tn),
        in_specs=[pl.BlockSpec((8, d), lambda l, j: (0, 0)),
                  pl.BlockSpec((1, d, tn), lambda l, j: (l, 0, j)),
                  pl.BlockSpec((1, 1, tn), lambda l, j: (l, 0, j))],
        out_specs=pl.BlockSpec((1, 8, tn), lambda l, j: (l, 0, j)),
        out_shape=jax.ShapeDtypeStruct((n_l, 8, d6), F32),
        compiler_params=_cp("parallel", "parallel"),
        name="modulation",
    )(s_rows, w_mod, b_mod.reshape(n_l, 1, d6))


def _inproj_kernel(x_ref, mod_ref, w_ref, wdt_ref, p_ref, dt_ref, h_scr):
    @pl.when(pl.program_id(2) == 0)
    def _():
        m = mod_ref[0]
        h = (x_ref[0] * (1.0 + m[1:2]) + m[0:1]).astype(BF16)
        h_scr[...] = h
        dt_ref[0] = _dot(h, wdt_ref[...])

    p_ref[0] = _dot(h_scr[...], w_ref[...]).astype(BF16)


def _inproj(x3, mod, w_main, w_dt, tm):
    g, r, d = x3.shape
    tn = 1152
    return pl.pallas_call(
        _inproj_kernel,
        grid=(g, r // tm, P_COLS // tn),
        in_specs=[pl.BlockSpec((1, tm, d), lambda b, i, j: (b, i, 0)),
                  pl.BlockSpec((1, 8, d), lambda b, i, j: (b, 0, 0)),
                  pl.BlockSpec((d, tn), lambda b, i, j: (0, j)),
                  pl.BlockSpec((d, DT_COLS), lambda b, i, j: (0, 0))],
        out_specs=[pl.BlockSpec((1, tm, tn), lambda b, i, j: (b, i, j)),
                   pl.BlockSpec((1, tm, DT_COLS), lambda b, i, j: (b, i, 0))],
        out_shape=[jax.ShapeDtypeStruct((g, r, P_COLS), BF16),
                   jax.ShapeDtypeStruct((g, r, DT_COLS), F32)],
        scratch_shapes=[pltpu.VMEM((tm, d), BF16)],
        compiler_params=_cp("parallel", "parallel", "arbitrary"),
        name="inproj",
    )(x3, mod, w_main, w_dt)


def _prep_kernel(swq_ref, gaq_ref, swk_ref, gak_ref, cos_ref, sa_ref, sb_ref, gq_ref, gk_ref,
                 oswq, ogaq, oswk, ogak, *, rope):
    row = lax.broadcasted_iota(jnp.int32, (128, 128), 0) // HEAD_DIM
    col = lax.broadcasted_iota(jnp.int32, (128, 128), 1) // HEAD_DIM
    head_mean = jnp.where(row == col, 1.0 / HEAD_DIM, 0.0).astype(F32)

    def rot(x):
        if not rope:
            return x
        return (x * cos_ref[...] + pltpu.roll(x, 128 - HEAD_DIM // 2, 1) * sa_ref[...]
                + pltpu.roll(x, HEAD_DIM // 2, 1) * sb_ref[...])

    def rms(x, g):
        ms = jnp.dot(x * x, head_mean, precision=HI, preferred_element_type=F32)
        return x * lax.rsqrt(ms + RMS_EPS) * g

    for c in range(4):
        sl = slice(c * 128, (c + 1) * 128)
        oswq[0, :, sl] = (rot(swq_ref[0, :, sl].astype(F32)) * ATTN_SCALE).astype(BF16)
        ogaq[0, :, sl] = (rot(rms(gaq_ref[0, :, sl].astype(F32), gq_ref[...])) * ATTN_SCALE).astype(BF16)
    oswk[0] = rot(swk_ref[0].astype(F32)).astype(BF16)
    ogak[0] = rot(rms(gak_ref[0].astype(F32), gk_ref[...])).astype(BF16)


def _prep(p, tables, gq, gk, tm, rope):
    g, r, _ = p.shape
    cosf, sin_a, sin_b = tables
    tab = pl.BlockSpec((tm, 128), lambda b, i: (i, 0))
    vec = pl.BlockSpec((1, 128), lambda b, i: (0, 0))
    return pl.pallas_call(
        functools.partial(_prep_kernel, rope=rope),
        grid=(g, r // tm),
        in_specs=[pl.BlockSpec((1, tm, 512), lambda b, i: (b, i, C512_SWQ)),
                  pl.BlockSpec((1, tm, 512), lambda b, i: (b, i, C512_GAQ)),
                  pl.BlockSpec((1, tm, 128), lambda b, i: (b, i, C128_SWK)),
                  pl.BlockSpec((1, tm, 128), lambda b, i: (b, i, C128_GAK)),
                  tab, tab, tab, vec, vec],
        out_specs=[pl.BlockSpec((1, tm, 512), lambda b, i: (b, i, 0)),
                   pl.BlockSpec((1, tm, 512), lambda b, i: (b, i, 0)),
                   pl.BlockSpec((1, tm, 128), lambda b, i: (b, i, 0)),
                   pl.BlockSpec((1, tm, 128), lambda b, i: (b, i, 0))],
        out_shape=[jax.ShapeDtypeStruct((g, r, 512), BF16), jax.ShapeDtypeStruct((g, r, 512), BF16),
                   jax.ShapeDtypeStruct((g, r, 128), BF16), jax.ShapeDtypeStruct((g, r, 128), BF16)],
        compiler_params=_cp("parallel", "parallel"),
        name="qk_prep",
    )(p, p, p, p, cosf, sin_a, sin_b, gq, gk)


def _rope_tables(n_tok):
    t = jnp.arange(n_tok, dtype=jnp.int32)
    row = (t // GRID_W).astype(F32)
    col = (t % GRID_W).astype(F32)
    n_freq = HEAD_DIM // 4
    inv_freq = ROPE_THETA ** (-jnp.arange(n_freq, dtype=F32) / n_freq)
    ang = jnp.concatenate([row[:, None] * inv_freq, col[:, None] * inv_freq], axis=-1)
    cos, sin = jnp.cos(ang), jnp.sin(ang)
    zero = jnp.zeros_like(sin)
    cosf = jnp.tile(jnp.concatenate([cos, cos], -1), (1, 2))
    sin_a = jnp.tile(jnp.concatenate([-sin, zero], -1), (1, 2))
    sin_b = jnp.tile(jnp.concatenate([zero, sin], -1), (1, 2))
    return cosf, sin_a, sin_b


def _conv_kernel(prev_ref, cur_ref, next_ref, w_ref, b_ref, o_ref, *, tc):
    i = pl.program_id(1)
    n = pl.num_programs(1)
    prev = jnp.where(i > 0, prev_ref[0].astype(F32), 0.0)
    nxt = jnp.where(i < n - 1, next_ref[0].astype(F32), 0.0)
    ext = jnp.concatenate([prev, cur_ref[0].astype(F32), nxt], axis=0)
    rows = tc + 32
    acc = b_ref[...] + jnp.zeros((tc, ext.shape[1]), F32)
    for j in range(SSM_CONV):
        shift = (SSM_CONV // 2 - j) % rows
        sh = ext if shift == 0 else pltpu.roll(ext, shift, 0)
        acc = acc + sh[16:16 + tc] * w_ref[j:j + 1, :]
    o_ref[0] = _silu(acc).astype(BF16)


def _conv(p, w8, b, tc):
    g, r, _ = p.shape
    nb16 = r // 16
    k = tc // 16
    return pl.pallas_call(
        functools.partial(_conv_kernel, tc=tc),
        grid=(g, r // tc),
        in_specs=[pl.BlockSpec((1, 16, 1024), lambda b_, i: (b_, jnp.maximum(i * k - 1, 0), C1024_XBC)),
                  pl.BlockSpec((1, tc, 1024), lambda b_, i: (b_, i, C1024_XBC)),
                  pl.BlockSpec((1, 16, 1024), lambda b_, i: (b_, jnp.minimum((i + 1) * k, nb16 - 1), C1024_XBC)),
                  pl.BlockSpec((8, 1024), lambda b_, i: (0, 0)),
                  pl.BlockSpec((1, 1024), lambda b_, i: (0, 0))],
        out_specs=pl.BlockSpec((1, tc, 1024), lambda b_, i: (b_, i, 0)),
        out_shape=jax.ShapeDtypeStruct((g, r, 1024), BF16),
        compiler_params=_cp("parallel", "parallel"),
        name="dwconv_silu",
    )(p, p, p, w8, b)


def _softmax_out(parts, sink=None):
    m = parts[0][0].max(axis=-1, keepdims=True)
    for s, _ in parts[1:]:
        m = jnp.maximum(m, s.max(axis=-1, keepdims=True))
    if sink is not None:
        m = jnp.maximum(m, sink)
    l = jnp.exp(sink - m) if sink is not None else 0.0
    acc = 0.0
    for s, v in parts:
        e = jnp.exp(s - m)
        l = l + e.sum(axis=-1, keepdims=True)
        acc = acc + _dot(e.astype(BF16), v)
    return acc / l


NA_ROWS_PER_STEP = 4


def _na_kernel(q_ref, k_ref, v_ref, kc_ref, vc_ref, bias_ref, o_ref, *, n_rows):
    blk = pl.program_id(2)
    kc = kc_ref[0]
    vc = vc_ref[0]

    def one_row(rr, carry):
        r = blk * NA_ROWS_PER_STEP + rr
        start = jnp.clip(r - NA_KH // 2, 0, n_rows - NA_KH)
        off = start - r + NA_KH - 1
        q = q_ref[0, pl.ds(pl.multiple_of(rr * GRID_W, GRID_W), GRID_W), :]
        k0 = pl.multiple_of(start * GRID_W, GRID_W)
        kw = k_ref[0, pl.ds(k0, NA_KH * GRID_W), :]
        vw = v_ref[0, pl.ds(k0, NA_KH * GRID_W), :]
        outs = []
        for hh in range(2):
            sl = slice(hh * HEAD_DIM, (hh + 1) * HEAD_DIM)
            qh = q[:, sl] * ATTN_SCALE
            s_loc = _dot_nt(qh, kw[:, sl]) + bias_ref[off, hh]
            s_ctx = _dot_nt(qh, kc[:, sl])
            outs.append(_softmax_out([(s_loc, vw[:, sl]), (s_ctx, vc[:, sl])]))
        o_ref[0, pl.ds(pl.multiple_of(rr * GRID_W, GRID_W), GRID_W), :] = jnp.concatenate(outs, axis=-1).astype(BF16)
        return carry

    lax.fori_loop(0, NA_ROWS_PER_STEP, one_row, 0)


def _na_bias_table(rpb):
    o = jnp.arange(NA_KH)
    kr = jnp.arange(NA_KH)
    dr = o[:, None] + kr[None, :]
    c = jnp.arange(GRID_W)
    kcol = jnp.arange(GRID_W)
    cs = jnp.clip(c - NA_KW // 2, 0, GRID_W - NA_KW)
    dc = kcol[None, :] - c[:, None] + NA_KW - 1
    valid = (kcol[None, :] >= cs[:, None]) & (kcol[None, :] < cs[:, None] + NA_KW)
    tbl = rpb[:, dr][:, :, :, jnp.clip(dc, 0, 2 * NA_KW - 2)]
    tbl = jnp.where(valid[None, None, None], tbl.astype(F32), NEG_INF)
    return tbl.transpose(1, 0, 3, 2, 4).reshape(NA_KH, NA_HEADS, GRID_W, NA_KH * GRID_W)


def _na_attention(p, pc, bias):
    b, s, _ = p.shape
    lc = pc.shape[1]
    n_rows = s // GRID_W
    tq = NA_ROWS_PER_STEP * GRID_W
    return pl.pallas_call(
        functools.partial(_na_kernel, n_rows=n_rows),
        grid=(b, NA_HEADS // 2, s // tq),
        in_specs=[pl.BlockSpec((1, tq, 128), lambda b_, h, i: (b_, i, 4 * C512_NAQ + h)),
                  pl.BlockSpec((1, s, 128), lambda b_, h, i: (b_, 0, 4 * C512_NAK + h)),
                  pl.BlockSpec((1, s, 128), lambda b_, h, i: (b_, 0, 4 * C512_NAV + h)),
                  pl.BlockSpec((1, lc, 128), lambda b_, h, i: (b_, 0, 4 * C512_NAK + h)),
                  pl.BlockSpec((1, lc, 128), lambda b_, h, i: (b_, 0, 4 * C512_NAV + h)),
                  pl.BlockSpec((NA_KH, 2, GRID_W, NA_KH * GRID_W), lambda b_, h, i: (0, h, 0, 0))],
        out_specs=pl.BlockSpec((1, tq, 128), lambda b_, h, i: (b_, i, h)),
        out_shape=jax.ShapeDtypeStruct((b, s, 512), BF16),
        compiler_params=_cp("parallel", "parallel", "arbitrary"),
        name="na_attention",
    )(p, p, p, pc, pc, bias)


SWA_TQ = 256
SWA_BAND = SWA_TQ + 2 * SWA_WINDOW


def _swa_kernel(q_ref, k_ref, v_ref, kc_ref, vc_ref, sink_ref, o_ref, *, n_tok):
    n = pl.program_id(1)
    start = pl.multiple_of(jnp.clip(n * SWA_TQ - SWA_WINDOW, 0, n_tok - SWA_BAND), SWA_WINDOW)
    kb = k_ref[0, pl.ds(start, SWA_BAND), :]
    vb = v_ref[0, pl.ds(start, SWA_BAND), :]
    kc = kc_ref[0]
    vc = vc_ref[0]
    qpos = n * SWA_TQ + lax.broadcasted_iota(jnp.int32, (SWA_TQ, SWA_BAND), 0)
    kpos = start + lax.broadcasted_iota(jnp.int32, (SWA_TQ, SWA_BAND), 1)
    in_win = jnp.abs(kpos - qpos) <= SWA_WINDOW
    outs = []
    for h in range(8):
        g = h // 4
        gs = slice(g * HEAD_DIM, (g + 1) * HEAD_DIM)
        qh = q_ref[0, :, h * HEAD_DIM:(h + 1) * HEAD_DIM]
        s_loc = jnp.where(in_win, _dot_nt(qh, kb[:, gs]), NEG_INF)
        s_ctx = _dot_nt(qh, kc[:, gs])
        outs.append(_softmax_out([(s_loc, vb[:, gs]), (s_ctx, vc[:, gs])], sink=sink_ref[h:h + 1, 0:1]))
    o_ref[0] = jnp.concatenate(outs, axis=-1).astype(BF16)


def _swa_attention(q, k, p, kc, pc, sink_tab):
    b, s, _ = q.shape
    lc = kc.shape[1]
    return pl.pallas_call(
        functools.partial(_swa_kernel, n_tok=s),
        grid=(b, s // SWA_TQ),
        in_specs=[pl.BlockSpec((1, SWA_TQ, 512), lambda b_, i: (b_, i, 0)),
                  pl.BlockSpec((1, s, 128), lambda b_, i: (b_, 0, 0)),
                  pl.BlockSpec((1, s, 128), lambda b_, i: (b_, 0, C128_SWV)),
                  pl.BlockSpec((1, lc, 128), lambda b_, i: (b_, 0, 0)),
                  pl.BlockSpec((1, lc, 128), lambda b_, i: (b_, 0, C128_SWV)),
                  pl.BlockSpec((8, 128), lambda b_, i: (0, 0))],
        out_specs=pl.BlockSpec((1, SWA_TQ, 512), lambda b_, i: (b_, i, 0)),
        out_shape=jax.ShapeDtypeStruct((b, s, 512), BF16),
        compiler_params=_cp("parallel", "arbitrary"),
        name="swa_attention",
    )(q, k, p, kc, pc, sink_tab)


GA_TQ = 256
GA_TK = 512


def _ga_kernel(q_ref, k_ref, v_ref, kc_ref, vc_ref, o_ref, *, n_tok):
    n_chunks = n_tok // GA_TK
    for h in range(8):
        g = h // 4
        gs = slice(g * HEAD_DIM, (g + 1) * HEAD_DIM)
        qh = q_ref[0, :, h * HEAD_DIM:(h + 1) * HEAD_DIM]

        def update(carry, kk, vv):
            m, l, acc = carry
            s = _dot_nt(qh, kk)
            m_new = jnp.maximum(m, s.max(axis=-1, keepdims=True))
            a = jnp.exp(m - m_new)
            e = jnp.exp(s - m_new)
            return (m_new, a * l + e.sum(axis=-1, keepdims=True), a * acc + _dot(e.astype(BF16), vv))

        def chunk(c, carry):
            k0 = pl.multiple_of(c * GA_TK, GA_TK)
            return update(carry, k_ref[0, pl.ds(k0, GA_TK), gs], v_ref[0, pl.ds(k0, GA_TK), gs])

        init = (jnp.full((GA_TQ, 1), NEG_INF, F32), jnp.zeros((GA_TQ, 1), F32), jnp.zeros((GA_TQ, HEAD_DIM), F32))
        carry = lax.fori_loop(0, n_chunks, chunk, init)
        m, l, acc = update(carry, kc_ref[0, :, gs], vc_ref[0, :, gs])
        o_ref[0, :, h * HEAD_DIM:(h + 1) * HEAD_DIM] = (acc / l).astype(BF16)


def _ga_attention(q, k, p, kc, pc):
    b, s, _ = q.shape
    lc = kc.shape[1]
    return pl.pallas_call(
        functools.partial(_ga_kernel, n_tok=s),
        grid=(b, s // GA_TQ),
        in_specs=[pl.BlockSpec((1, GA_TQ, 512), lambda b_, i: (b_, i, 0)),
                  pl.BlockSpec((1, s, 128), lambda b_, i: (b_, 0, 0)),
                  pl.BlockSpec((1, s, 128), lambda b_, i: (b_, 0, C128_GAV)),
                  pl.BlockSpec((1, lc, 128), lambda b_, i: (b_, 0, 0)),
                  pl.BlockSpec((1, lc, 128), lambda b_, i: (b_, 0, C128_GAV))],
        out_specs=pl.BlockSpec((1, GA_TQ, 512), lambda b_, i: (b_, i, 0)),
        out_shape=jax.ShapeDtypeStruct((b, s, 512), BF16),
        compiler_params=_cp("parallel", "arbitrary"),
        name="ga_attention",
    )(q, k, p, kc, pc)


def _ctx_attn_kernel(q_ref, k_ref, v_ref, sink_ref, o_ref, *, n_kv, q_scale, use_sink):
    rep = 8 // n_kv
    outs = []
    for h in range(8):
        g = h // rep
        gs = slice(g * HEAD_DIM, (g + 1) * HEAD_DIM)
        qh = q_ref[0, :, h * HEAD_DIM:(h + 1) * HEAD_DIM]
        if q_scale:
            qh = qh * ATTN_SCALE
        s = _dot_nt(qh, k_ref[0, :, gs])
        sink = sink_ref[h:h + 1, 0:1] if use_sink else None
        outs.append(_softmax_out([(s, v_ref[0, :, gs])], sink=sink))
    o_ref[0] = jnp.concatenate(outs, axis=-1).astype(BF16)


def _ctx_attention(q, qcol, k, kcol, v, vcol, n_kv, sink_tab, q_scale, use_sink):
    b, lc, _ = q.shape
    kvw = n_kv * HEAD_DIM
    return pl.pallas_call(
        functools.partial(_ctx_attn_kernel, n_kv=n_kv, q_scale=q_scale, use_sink=use_sink),
        grid=(b,),
        in_specs=[pl.BlockSpec((1, lc, 512), lambda b_: (b_, 0, qcol)),
                  pl.BlockSpec((1, lc, kvw), lambda b_: (b_, 0, kcol)),
                  pl.BlockSpec((1, lc, kvw), lambda b_: (b_, 0, vcol)),
                  pl.BlockSpec((8, 128), lambda b_: (0, 0))],
        out_specs=pl.BlockSpec((1, lc, 512), lambda b_: (b_, 0, 0)),
        out_shape=jax.ShapeDtypeStruct((b, lc, 512), BF16),
        compiler_params=_cp("parallel"),
        name="ctx_attention",
    )(q, k, v, sink_tab)


def _ssd_kernel(u_ref, dt_ref, dtb_ref, a_ref, h0_ref, y_ref, hT_ref, st, *, q_len):
    d = pl.program_id(1)
    c = pl.program_id(2)

    @pl.when(c == 0)
    def _():
        st[...] = h0_ref[0, 0]

    u = u_ref[0]
    dt = _softplus(dt_ref[0] + dtb_ref[...])
    da = dt * a_ref[...]
    row = lax.broadcasted_iota(jnp.int32, (q_len, q_len), 0)
    col = lax.broadcasted_iota(jnp.int32, (q_len, q_len), 1)
    causal = (row - col) * (1 - 2 * d) >= 0
    acum = jnp.dot(causal.astype(F32), da, precision=HI, preferred_element_type=F32)
    acum_t = acum.T
    total = jnp.sum(da, axis=0, keepdims=True)
    ys = []
    for g in range(2):
        bg = u[:, SSM_INNER + g * SSM_STATE:SSM_INNER + (g + 1) * SSM_STATE]
        cg = u[:, SSM_INNER + (2 + g) * SSM_STATE:SSM_INNER + (3 + g) * SSM_STATE]
        cb = _dot_nt(cg, bg)
        bg_t = bg.astype(F32).T.astype(BF16)
        s_g = st[g]
        y_off = _dot(cg, s_g.astype(BF16))
        new_cols = []
        for hh in range(4):
            h = g * 4 + hh
            hs = slice(hh * HEAD_DIM, (hh + 1) * HEAD_DIM)
            ac = acum[:, h:h + 1]
            seg = jnp.exp(jnp.where(causal, ac - acum_t[h:h + 1, :], NEG_INF))
            xdt = u[:, h * HEAD_DIM:(h + 1) * HEAD_DIM].astype(F32) * dt[:, h:h + 1]
            y_h = _dot((cb * seg).astype(BF16), xdt.astype(BF16)) + y_off[:, hs] * jnp.exp(ac)
            ys.append(y_h)
            tot = total[:, h:h + 1]
            xdec = (xdt * jnp.exp(tot - ac)).astype(BF16)
            new_cols.append(s_g[:, hs] * jnp.exp(tot) + _dot(bg_t, xdec))
        st[g] = jnp.concatenate(new_cols, axis=-1)
    y_ref[0, 0] = jnp.concatenate(ys, axis=-1)

    @pl.when(c == pl.num_programs(2) - 1)
    def _():
        hT_ref[0, 0] = st[...]


def _ssd(u, dt_raw, dtb, a_full, h0, q_len):
    b, l, _ = u.shape
    n_c = l // q_len

    def cidx(d, c):
        return c + d * (n_c - 1 - 2 * c)

    return pl.pallas_call(
        functools.partial(_ssd_kernel, q_len=q_len),
        grid=(b, 2, n_c),
        in_specs=[pl.BlockSpec((1, q_len, 1024), lambda b_, d, c: (b_, cidx(d, c), 0)),
                  pl.BlockSpec((1, q_len, 128), lambda b_, d, c: (b_, cidx(d, c), d)),
                  pl.BlockSpec((1, 128), lambda b_, d, c: (0, d)),
                  pl.BlockSpec((1, 128), lambda b_, d, c: (0, d)),
                  pl.BlockSpec((1, 1, 2, SSM_STATE, 256), lambda b_, d, c: (b_, d, 0, 0, 0))],
        out_specs=[pl.BlockSpec((1, 1, q_len, 512), lambda b_, d, c: (d, b_, cidx(d, c), 0)),
                   pl.BlockSpec((1, 1, 2, SSM_STATE, 256), lambda b_, d, c: (b_, d, 0, 0, 0))],
        out_shape=[jax.ShapeDtypeStruct((2, b, l, 512), F32),
                   jax.ShapeDtypeStruct((b, 2, 2, SSM_STATE, 256), F32)],
        scratch_shapes=[pltpu.VMEM((2, SSM_STATE, 256), F32)],
        compiler_params=_cp("parallel", "arbitrary", "arbitrary"),
        name="ssd_scan",
    )(u, dt_raw, dtb, a_full, h0)


def _ssd_finish_kernel(yf_ref, yb_ref, xs_ref, z_ref, dsk_ref, g_ref, o_ref):
    y = yf_ref[0, 0] + yb_ref[0, 0] + dsk_ref[...] * xs_ref[0].astype(F32)
    y = y * _silu(z_ref[0].astype(F32))
    ms = jnp.mean(y * y, axis=-1, keepdims=True)
    o_ref[0] = (y * lax.rsqrt(ms + RMS_EPS) * g_ref[...]).astype(BF16)


def _ssd_finish(y, u, p, dsk, g, tm):
    _, b, l, _ = y.shape
    vec = pl.BlockSpec((1, 512), lambda b_, i: (0, 0))
    return pl.pallas_call(
        _ssd_finish_kernel,
        grid=(b, l // tm),
        in_specs=[pl.BlockSpec((1, 1, tm, 512), lambda b_, i: (0, b_, i, 0)),
                  pl.BlockSpec((1, 1, tm, 512), lambda b_, i: (1, b_, i, 0)),
                  pl.BlockSpec((1, tm, 512), lambda b_, i: (b_, i, 0)),
                  pl.BlockSpec((1, tm, 512), lambda b_, i: (b_, i, C512_Z)),
                  vec, vec],
        out_specs=pl.BlockSpec((1, tm, 512), lambda b_, i: (b_, i, 0)),
        out_shape=jax.ShapeDtypeStruct((b, l, 512), BF16),
        compiler_params=_cp("parallel", "parallel"),
        name="ssd_finish",
    )(y, y, u, p, dsk, g)


def _merge_kernel(*refs, alpha, moe):
    if moe:
        (x_ref, mod_ref, oa, ob, oc, od, wg_ref, bg_ref, wb_ref, wo_ref, lng_ref, lnb_ref, wr_ref, br_ref,
         x1_ref, h2_ref, lg_ref, h_scr, acc_scr) = refs
    else:
        (x_ref, mod_ref, oa, ob, oc, od, wg_ref, bg_ref, wb_ref, wo_ref, lng_ref, lnb_ref,
         x1_ref, h_scr, acc_scr) = refs
    i = pl.program_id(2)
    m = mod_ref[0]

    @pl.when(i == 0)
    def _():
        h_scr[...] = (x_ref[0] * (1.0 + m[1:2]) + m[0:1]).astype(BF16)
        acc_scr[...] = jnp.zeros_like(acc_scr)

    for k, br in enumerate((oa, ob, oc, od)):
        @pl.when(i == k)
        def _(br=br):
            gate = _sigmoid(_dot(h_scr[...], wg_ref[0]) + bg_ref[0])
            acc_scr[...] += gate * _dot(br[0], wb_ref[0])

    @pl.when(i == N_BRANCH - 1)
    def _():
        o = _dot(acc_scr[...].astype(BF16), wo_ref[...])
        x1 = _layer_norm(alpha * x_ref[0] + m[2:3] * o, lng_ref[...], lnb_ref[...])
        x1_ref[0] = x1
        if moe:
            h2 = x1 * (1.0 + m[4:5]) + m[3:4]
            h2_ref[0] = h2
            lg_ref[0] = jnp.dot(h2, wr_ref[...], precision=HI, preferred_element_type=F32) + br_ref[...]


def _merge(x3, mod, branches, wg, bg, wb, wo, lng, lnb, alpha, tm, router=None):
    g, r, d = x3.shape
    moe = router is not None
    tile = lambda w: pl.BlockSpec((1, tm, w), lambda b, t, i: (b, t, 0))
    vec = pl.BlockSpec((1, d), lambda b, t, i: (0, 0))
    in_specs = [tile(d), pl.BlockSpec((1, 8, d), lambda b, t, i: (b, 0, 0)),
                tile(BRANCH_W), tile(BRANCH_W), tile(BRANCH_W), tile(BRANCH_W),
                pl.BlockSpec((1, d, d), lambda b, t, i: (i, 0, 0)),
                pl.BlockSpec((1, 1, d), lambda b, t, i: (i, 0, 0)),
                pl.BlockSpec((1, BRANCH_W, d), lambda b, t, i: (i, 0, 0)),
                pl.BlockSpec((d, d), lambda b, t, i: (0, 0)), vec, vec]
    args = [x3, mod, *branches, wg, bg, wb, wo, lng, lnb]
    out_specs = [tile(d)]
    out_shape = [jax.ShapeDtypeStruct((g, r, d), F32)]
    if moe:
        in_specs += [pl.BlockSpec((d, 128), lambda b, t, i: (0, 0)), pl.BlockSpec((1, 128), lambda b, t, i: (0, 0))]
        args += list(router)
        out_specs += [tile(d), tile(128)]
        out_shape += [jax.ShapeDtypeStruct((g, r, d), F32), jax.ShapeDtypeStruct((g, r, 128), F32)]
    return pl.pallas_call(
        functools.partial(_merge_kernel, alpha=alpha, moe=moe),
        grid=(g, r // tm, N_BRANCH),
        in_specs=in_specs, out_specs=out_specs, out_shape=out_shape,
        scratch_shapes=[pltpu.VMEM((tm, d), BF16), pltpu.VMEM((tm, d), F32)],
        compiler_params=_cp("parallel", "parallel", "arbitrary"),
        name="merge_ln1",
    )(*args)


FFN_CHUNK = D_FF // 2


def _ffn_kernel(x_ref, mod_ref, wg_ref, wu_ref, wd_ref, lng_ref, lnb_ref, o_ref, h_scr, acc_scr, *, alpha):
    j = pl.program_id(2)
    m = mod_ref[0]

    @pl.when(j == 0)
    def _():
        h_scr[...] = (x_ref[0] * (1.0 + m[4:5]) + m[3:4]).astype(BF16)
        acc_scr[...] = jnp.zeros_like(acc_scr)

    h = h_scr[...]
    a = _silu(_dot(h, wg_ref[...])) * _dot(h, wu_ref[...])
    acc_scr[...] += _dot(a.astype(BF16), wd_ref[...])

    @pl.when(j == pl.num_programs(2) - 1)
    def _():
        o_ref[0] = _layer_norm(alpha * x_ref[0] + m[5:6] * acc_scr[...], lng_ref[...], lnb_ref[...])


def _ffn(x3, mod, w_up, w_down, lng, lnb, alpha, tm):
    g, r, d = x3.shape
    n_j = D_FF // FFN_CHUNK
    vec = pl.BlockSpec((1, d), lambda b, t, j: (0, 0))
    return pl.pallas_call(
        functools.partial(_ffn_kernel, alpha=alpha),
        grid=(g, r // tm, n_j),
        in_specs=[pl.BlockSpec((1, tm, d), lambda b, t, j: (b, t, 0)),
                  pl.BlockSpec((1, 8, d), lambda b, t, j: (b, 0, 0)),
                  pl.BlockSpec((d, FFN_CHUNK), lambda b, t, j: (0, j)),
                  pl.BlockSpec((d, FFN_CHUNK), lambda b, t, j: (0, n_j + j)),
                  pl.BlockSpec((FFN_CHUNK, d), lambda b, t, j: (j, 0)), vec, vec],
        out_specs=pl.BlockSpec((1, tm, d), lambda b, t, j: (b, t, 0)),
        out_shape=jax.ShapeDtypeStruct((g, r, d), F32),
        scratch_shapes=[pltpu.VMEM((tm, d), BF16), pltpu.VMEM((tm, d), F32)],
        compiler_params=_cp("parallel", "parallel", "arbitrary"),
        name="ffn_ln2",
    )(x3, mod, w_up, w_up, w_down, lng, lnb)


ROUTE_TM = 256


def _route_kernel(lg_ref, info_ref, cnt_ref, carry):
    i = pl.program_id(0)

    @pl.when(i == 0)
    def _():
        carry[...] = jnp.zeros_like(carry)

    lg = lg_ref[...]
    tm = lg.shape[0]
    lane = lax.broadcasted_iota(jnp.int32, lg.shape, 1).astype(F32)
    m1 = lg.max(axis=-1, keepdims=True)
    i1 = jnp.where(lg == m1, lane, 128.0).min(axis=-1, keepdims=True)
    lg2 = jnp.where(lane == i1, -jnp.inf, lg)
    m2 = lg2.max(axis=-1, keepdims=True)
    i2 = jnp.where(lg2 == m2, lane, 128.0).min(axis=-1, keepdims=True)
    e = jnp.exp(m2 - m1)
    w1 = 1.0 / (1.0 + e)
    w2 = e / (1.0 + e)
    sel = jnp.where((lane == i1) | (lane == i2), 1.0, 0.0).astype(F32)
    row = lax.broadcasted_iota(jnp.int32, (tm, tm), 0)
    col = lax.broadcasted_iota(jnp.int32, (tm, tm), 1)
    before = jnp.where(row > col, 1.0, 0.0).astype(BF16)
    excl = _dot(before, sel.astype(BF16)) + carry[0:1, :]
    r1 = jnp.where(lane == i1, excl, 0.0).sum(axis=-1, keepdims=True)
    r2 = jnp.where(lane == i2, excl, 0.0).sum(axis=-1, keepdims=True)
    carry[...] = carry[...] + sel.sum(axis=0, keepdims=True)
    info = jnp.where(lane == 0, i1, 0.0)
    info = jnp.where(lane == 1, i2, info)
    info = jnp.where(lane == 2, r1, info)
    info = jnp.where(lane == 3, r2, info)
    info = jnp.where(lane == 4, w1, info)
    info = jnp.where(lane == 5, w2, info)
    info_ref[...] = info
    cnt_ref[...] = carry[...]


def _route(logits):
    t = logits.shape[0]
    return pl.pallas_call(
        _route_kernel,
        grid=(t // ROUTE_TM,),
        in_specs=[pl.BlockSpec((ROUTE_TM, 128), lambda i: (i, 0))],
        out_specs=[pl.BlockSpec((ROUTE_TM, 128), lambda i: (i, 0)), pl.BlockSpec((8, 128), lambda i: (0, 0))],
        out_shape=[jax.ShapeDtypeStruct((t, 128), F32), jax.ShapeDtypeStruct((8, 128), F32)],
        scratch_shapes=[pltpu.VMEM((8, 128), F32)],
        compiler_params=_cp("arbitrary"),
        name="moe_route",
    )(logits)


SCATTER_TM = 256


def _scatter_kernel(d_ref, tok_ref, zero_ref, out_ref, sem):
    del zero_ref
    base = pl.program_id(0) * SCATTER_TM

    def copy(t, k):
        return pltpu.make_async_copy(tok_ref.at[pl.ds(base + t, 1)], out_ref.at[pl.ds(d_ref[0, 0, 2 * t + k], 1)], sem)

    def issue(t, carry):
        copy(t, 0).start()
        copy(t, 1).start()
        return carry

    def drain(t, carry):
        copy(t, 0).wait()
        copy(t, 1).wait()
        return carry

    lax.fori_loop(0, SCATTER_TM, issue, 0)
    lax.fori_loop(0, SCATTER_TM, drain, 0)


def _scatter_rows(dest, tok, n_pad):
    t, d = tok.shape
    n_t = t // SCATTER_TM
    return pl.pallas_call(
        _scatter_kernel,
        grid=(n_t,),
        in_specs=[pl.BlockSpec((1, 1, 2 * SCATTER_TM), lambda i: (i, 0, 0), memory_space=pltpu.SMEM),
                  pl.BlockSpec(memory_space=pl.ANY),
                  pl.BlockSpec(memory_space=pl.ANY)],
        out_specs=pl.BlockSpec(memory_space=pl.ANY),
        out_shape=jax.ShapeDtypeStruct((n_pad, d), F32),
        scratch_shapes=[pltpu.SemaphoreType.DMA(())],
        input_output_aliases={2: 0},
        compiler_params=_cp("arbitrary"),
        name="moe_scatter",
    )(dest.reshape(n_t, 1, 2 * SCATTER_TM), tok, jnp.zeros((n_pad, d), F32))


EXP_CHUNK = D_FF_EXPERT // 4


def _expert_kernel(be_ref, nv_ref, x_ref, wg_ref, wu_ref, wd_ref, y_ref):
    del be_ref
    b = pl.program_id(0)
    j = pl.program_id(1)

    @pl.when(j == 0)
    def _():
        y_ref[...] = jnp.zeros_like(y_ref)

    @pl.when(b < nv_ref[0])
    def _():
        h = x_ref[...].astype(BF16)
        a = _silu(_dot(h, wg_ref[0])) * _dot(h, wu_ref[0])
        y_ref[...] += _dot(a.astype(BF16), wd_ref[0])


def _experts(x_pad, blk_expert, n_valid, w_up, w_down):
    n_pad, d = x_pad.shape
    n_blk = n_pad // MOE_BLOCK
    n_j = D_FF_EXPERT // EXP_CHUNK

    def jj(b, j, nv):
        return jnp.where(b < nv[0], j, n_j - 1)

    grid_spec = pltpu.PrefetchScalarGridSpec(
        num_scalar_prefetch=2,
        grid=(n_blk, n_j),
        in_specs=[pl.BlockSpec((MOE_BLOCK, d), lambda b, j, be, nv: (b, 0)),
                  pl.BlockSpec((1, d, EXP_CHUNK), lambda b, j, be, nv: (be[b], 0, jj(b, j, nv))),
                  pl.BlockSpec((1, d, EXP_CHUNK), lambda b, j, be, nv: (be[b], 0, n_j + jj(b, j, nv))),
                  pl.BlockSpec((1, EXP_CHUNK, d), lambda b, j, be, nv: (be[b], jj(b, j, nv), 0))],
        out_specs=pl.BlockSpec((MOE_BLOCK, d), lambda b, j, be, nv: (b, 0)),
    )
    return pl.pallas_call(
        _expert_kernel,
        grid_spec=grid_spec,
        out_shape=jax.ShapeDtypeStruct((n_pad, d), F32),
        compiler_params=_cp("arbitrary", "arbitrary"),
        name="moe_experts",
    )(blk_expert, n_valid, x_pad, w_up, w_up, w_down)


COMBINE_TM = 256


def _combine_kernel(d_ref, info_ref, x_ref, mod_ref, lng_ref, lnb_ref, y_ref, o_ref, gbuf, sem, *, alpha):
    def copy(t, k):
        return pltpu.make_async_copy(y_ref.at[pl.ds(d_ref[0, 0, 2 * t + k], 1)], gbuf.at[k, pl.ds(t, 1)], sem)

    def issue(t, carry):
        copy(t, 0).start()
        copy(t, 1).start()
        return carry

    def drain(t, carry):
        copy(t, 0).wait()
        copy(t, 1).wait()
        return carry

    lax.fori_loop(0, COMBINE_TM, issue, 0)
    lax.fori_loop(0, COMBINE_TM, drain, 0)
    m = mod_ref[0]
    info = info_ref[0]
    f = info[:, 4:5] * gbuf[0] + info[:, 5:6] * gbuf[1]
    o_ref[0] = _layer_norm(alpha * x_ref[0] + m[5:6] * f, lng_ref[...], lnb_ref[...])


def _combine(dest, info, x3, mod, lng, lnb, y_pad, alpha):
    g, r, d = x3.shape
    n_t = r // COMBINE_TM
    vec = pl.BlockSpec((1, d), lambda b, i: (0, 0))
    return pl.pallas_call(
        functools.partial(_combine_kernel, alpha=alpha),
        grid=(g, n_t),
        in_specs=[pl.BlockSpec((1, 1, 2 * COMBINE_TM), lambda b, i: (b * n_t + i, 0, 0), memory_space=pltpu.SMEM),
                  pl.BlockSpec((1, COMBINE_TM, 128), lambda b, i: (b, i, 0)),
                  pl.BlockSpec((1, COMBINE_TM, d), lambda b, i: (b, i, 0)),
                  pl.BlockSpec((1, 8, d), lambda b, i: (b, 0, 0)), vec, vec,
                  pl.BlockSpec(memory_space=pl.ANY)],
        out_specs=pl.BlockSpec((1, COMBINE_TM, d), lambda b, i: (b, i, 0)),
        out_shape=jax.ShapeDtypeStruct((g, r, d), F32),
        scratch_shapes=[pltpu.VMEM((2, COMBINE_TM, d), F32), pltpu.SemaphoreType.DMA(())],
        compiler_params=_cp("arbitrary", "arbitrary"),
        name="moe_combine_ln2",
    )(dest.reshape(g * n_t, 1, 2 * COMBINE_TM), info.reshape(g, r, 128), x3, mod, lng, lnb, y_pad)


def _moe_plan(info, counts_row):
    n_tok = info.shape[0]
    e = info[:, 0:2].astype(jnp.int32)
    rank = info[:, 2:4].astype(jnp.int32)
    counts = counts_row[:N_EXPERTS].astype(jnp.int32)
    padded = (counts + MOE_BLOCK - 1) // MOE_BLOCK * MOE_BLOCK
    pad_end = jnp.cumsum(padded)
    pad_start = pad_end - padded
    dest = pad_start[e] + rank
    n_blk = -(-n_tok * 2 // MOE_BLOCK) + N_EXPERTS
    blk_expert = jnp.minimum(
        jnp.searchsorted(pad_end, jnp.arange(n_blk, dtype=jnp.int32) * MOE_BLOCK, side='right'),
        N_EXPERTS - 1).astype(jnp.int32)
    n_valid = (pad_end[-1:] // MOE_BLOCK).astype(jnp.int32)
    return dest, blk_expert, n_valid, n_blk * MOE_BLOCK


def _pad_rows(a, rows):
    return jnp.concatenate([a, jnp.zeros((rows - a.shape[0],) + a.shape[1:], a.dtype)], axis=0)


def kernel(x, c, ctx, c_ctx, w_mod, b_mod, w_in, w_gate, b_gate, na_rpb, swa_sink, qk_gain_q, qk_gain_k, conv_w, conv_b, dt_bias, a_log, d_skip, ssm_norm_g, w_branch, w_out, ln1_g, ln1_b, ln2_g, ln2_b, ffn_w_up, ffn_w_down, moe_w_router, moe_b_router, moe_w_up, moe_w_down):
    bsz, seq, d = x.shape
    lc = ctx.shape[1]
    depth = w_mod.shape[0]
    alpha = (2 * depth) ** 0.25
    n_ctx = bsz * lc
    assert bsz < 8 and seq % 1024 == 0 and lc % 256 == 0 and d == D_MODEL

    s_rows = _pad_rows(jnp.concatenate([c, c_ctx[None, :]], axis=0), 8)
    mods = _modulation(s_rows, w_mod, b_mod).reshape(depth, 8, 6, d)
    mods = jnp.concatenate([mods, jnp.zeros((depth, 8, 2, d), F32)], axis=2)

    offs = [0, 512, 1024, 1536, 2048, 2176, 2304, 2816, 2944, 3072, 3584, 4608, 4624]
    seg = lambda k: w_in[:, :, offs[k]:offs[k + 1]]
    w_main = jnp.concatenate([seg(0), seg(1), seg(2), seg(3), seg(6), seg(9), seg(10), seg(4), seg(5), seg(7), seg(8)],
                             axis=-1).astype(BF16)
    zpad = jnp.zeros((depth, d, 128 - SSM_HEADS), F32)
    w_dt = jnp.concatenate([seg(11)[:, :, :SSM_HEADS], zpad, seg(11)[:, :, SSM_HEADS:], zpad], axis=-1).astype(BF16)

    w_gate_b = w_gate.astype(BF16)
    w_branch_b = w_branch.astype(BF16)
    w_out_b = w_out.astype(BF16)
    ffn_up_b = ffn_w_up.astype(BF16)
    ffn_down_b = ffn_w_down.astype(BF16)
    moe_up_b = moe_w_up.astype(BF16)
    moe_down_b = moe_w_down.astype(BF16)

    rope_tabs = _rope_tables(seq)
    a_neg = -jnp.exp(a_log.astype(F32))
    pad8 = lambda v: jnp.concatenate([v, jnp.zeros(v.shape[:-1] + (128 - SSM_HEADS,), F32)], axis=-1)

    ctx2 = ctx.reshape(1, n_ctx, d)
    for l in range(depth):
        with_ctx = l < depth - 1
        mod_lat = mods[l, :bsz]
        mod_ctx = mods[l, bsz:bsz + 1]

        p, dt_raw = _inproj(x, mod_lat, w_main[l], w_dt[l], 1024)
        pc, dtc_raw = _inproj(ctx2, mod_ctx, w_main[l], w_dt[l], n_ctx)
        pc = pc.reshape(bsz, lc, P_COLS)
        dtc_raw = dtc_raw.reshape(bsz, lc, DT_COLS)

        gq = jnp.tile(qk_gain_q[l].astype(F32), 2)[None, :]
        gk = jnp.tile(qk_gain_k[l].astype(F32), 2)[None, :]
        swq, gaq, swk, gak = _prep(p, rope_tabs, gq, gk, 1024, True)
        swq_c, gaq_c, swk_c, gak_c = _prep(pc, rope_tabs, gq, gk, lc, False)

        sink_tab = jnp.broadcast_to(swa_sink[l].astype(F32)[:, None], (8, 128))
        o_a = _na_attention(p, pc, _na_bias_table(na_rpb[l]))
        o_b = _swa_attention(swq, swk, p, swk_c, pc, sink_tab)
        o_c = _ga_attention(gaq, gak, p, gak_c, pc)

        w8 = _pad_rows(conv_w[l].astype(F32), 8)
        cb = conv_b[l].astype(F32)[None, :]
        u = _conv(p, w8, cb, 512)
        uc = _conv(pc, w8, cb, lc)
        dtb = jnp.concatenate([pad8(dt_bias[l, 0].astype(F32)), pad8(dt_bias[l, 1].astype(F32))])[None, :]
        a_full = jnp.concatenate([pad8(a_neg[l, 0]), pad8(a_neg[l, 1])])[None, :]
        h_zero = jnp.zeros((bsz, 2, 2, SSM_STATE, 256), F32)
        yc, h_ctx = _ssd(uc, dtc_raw, dtb, a_full, h_zero, lc)
        y, _ = _ssd(u, dt_raw, dtb, a_full, h_ctx, 256)
        dsk = jnp.repeat(d_skip[l].astype(F32), HEAD_DIM)[None, :]
        ng = ssm_norm_g[l].astype(F32)[None, :]
        o_d = _ssd_finish(y, u, p, dsk, ng, 1024)

        is_moe = l % 2 == 1
        i = l // 2
        router = None
        if is_moe:
            wr = jnp.concatenate([moe_w_router[i].astype(F32), jnp.zeros((d, 128 - N_EXPERTS), F32)], axis=-1)
            br = jnp.concatenate([moe_b_router[i].astype(F32), jnp.full((128 - N_EXPERTS,), NEG_INF, F32)])[None, :]
            router = (wr, br)
        lng1, lnb1 = ln1_g[l].astype(F32)[None, :], ln1_b[l].astype(F32)[None, :]
        lng2, lnb2 = ln2_g[l].astype(F32)[None, :], ln2_b[l].astype(F32)[None, :]
        bg = b_gate[l].astype(F32)[:, None, :]
        res = _merge(x, mod_lat, (o_a, o_b, o_c, o_d), w_gate_b[l], bg, w_branch_b[l], w_out_b[l], lng1, lnb1,
                     alpha, 512, router)
        if with_ctx:
            o_ac = _ctx_attention(pc, C512_NAQ, pc, C512_NAK, pc, C512_NAV, 8, sink_tab, True, False)
            o_bc = _ctx_attention(swq_c, 0, swk_c, 0, pc, C128_SWV, 2, sink_tab, False, True)
            o_cc = _ctx_attention(gaq_c, 0, gak_c, 0, pc, C128_GAV, 2, sink_tab, False, False)
            o_dc = _ssd_finish(yc, uc, pc, dsk, ng, lc)
            to2 = lambda t: t.reshape(1, n_ctx, BRANCH_W)
            res_c = _merge(ctx2, mod_ctx, (to2(o_ac), to2(o_bc), to2(o_cc), to2(o_dc)), w_gate_b[l], bg,
                           w_branch_b[l], w_out_b[l], lng1, lnb1, alpha, 512, router)

        if not is_moe:
            x = _ffn(res[0], mod_lat, ffn_up_b[i], ffn_down_b[i], lng2, lnb2, alpha, 1024)
            if with_ctx:
                ctx2 = _ffn(res_c[0], mod_ctx, ffn_up_b[i], ffn_down_b[i], lng2, lnb2, alpha, n_ctx)
        else:
            x1, h2, lg = res
            tok = h2.reshape(bsz * seq, d)
            lgs = lg.reshape(bsz * seq, 128)
            if with_ctx:
                x1c, h2c, lgc = res_c
                tok = jnp.concatenate([tok, h2c.reshape(n_ctx, d)], axis=0)
                lgs = jnp.concatenate([lgs, lgc.reshape(n_ctx, 128)], axis=0)
            info, cnt = _route(lgs)
            dest, blk_expert, n_valid, n_pad = _moe_plan(info, cnt[0])
            x_pad = _scatter_rows(dest, tok, n_pad)
            y_pad = _experts(x_pad, blk_expert, n_valid, moe_up_b[i], moe_down_b[i])
            n_lat = bsz * seq
            x = _combine(dest[:n_lat], info[:n_lat], x1, mod_lat, lng2, lnb2, y_pad, alpha)
            if with_ctx:
                ctx2 = _combine(dest[n_lat:], info[n_lat:], x1c, mod_ctx, lng2, lnb2, y_pad, alpha)
    return x
```

```python
import functools
import math

import jax
import jax.numpy as jnp
from jax import lax
from jax.experimental import pallas as pl
from jax.experimental.pallas import tpu as pltpu

F32 = jnp.float32
BF16 = jnp.bfloat16
HI = lax.Precision.HIGHEST

D_MODEL = 1024
GRID_W = 64
HEAD_DIM = 64
ATTN_SCALE = HEAD_DIM ** -0.5
ROPE_THETA = 10000.0
NEG_INF = -1e30
NA_HEADS = 8
NA_KH = 8
NA_KW = 16
SWA_WINDOW = 128
SSM_HEADS = 8
SSM_INNER = 512
SSM_STATE = 128
SSM_CONV = 5
N_BRANCH = 4
BRANCH_W = 512
D_FF = 2816
N_EXPERTS = 8
D_FF_EXPERT = 3584
MOE_BLOCK = 512
LN_EPS = 1e-5
RMS_EPS = 1e-6

P_COLS = 4608
C512_NAQ, C512_NAK, C512_NAV, C512_SWQ, C512_GAQ, C512_Z = 0, 1, 2, 3, 4, 5
C1024_XBC = 3
C128_SWK, C128_SWV, C128_GAK, C128_GAV = 32, 33, 34, 35
DT_COLS = 256

VMEM_LIMIT = 48 * 1024 * 1024


def _cp(*sem):
    return pltpu.CompilerParams(dimension_semantics=sem, vmem_limit_bytes=VMEM_LIMIT)


def _sigmoid(x):
    return 1.0 / (1.0 + jnp.exp(-x))


def _silu(x):
    return x * _sigmoid(x)


def _softplus(x):
    return jnp.maximum(x, 0.0) + jnp.log1p(jnp.exp(-jnp.abs(x)))


def _layer_norm(y, g, b):
    mu = jnp.mean(y, axis=-1, keepdims=True)
    yc = y - mu
    var = jnp.mean(yc * yc, axis=-1, keepdims=True)
    return yc * lax.rsqrt(var + LN_EPS) * g + b


def _dot(a, b):
    return jnp.dot(a, b, preferred_element_type=F32)


def _dot_nt(a, b):
    return lax.dot_general(a, b, (((1,), (1,)), ((), ())), preferred_element_type=F32)


def _mod_kernel(s_ref, w_ref, b_ref, o_ref):
    s = _silu(s_ref[...])
    o_ref[0] = jnp.dot(s, w_ref[0], precision=HI, preferred_element_type=F32) + b_ref[0]


def _modulation(s_rows, w_mod, b_mod):
    n_l, d, d6 = w_mod.shape
    tn = 1024
    return pl.pallas_call(
        _mod_kernel,
        grid=(n_l, d6 // tn),
        in_specs=[pl.BlockSpec((8, d), lambda l, j: (0, 0)),
                  pl.BlockSpec((1, d, tn), lambda l, j: (l, 0, j)),
                  pl.BlockSpec((1, 1, tn), lambda l, j: (l, 0, j))],
        out_specs=pl.BlockSpec((1, 8, tn), lambda l, j: (l, 0, j)),
        out_shape=jax.ShapeDtypeStruct((n_l, 8, d6), F32),
        compiler_params=_cp("parallel", "parallel"),
        name="modulation",
    )(s_rows, w_mod, b_mod.reshape(n_l, 1, d6))


def _inproj_kernel(x_ref, mod_ref, w_ref, wdt_ref, p_ref, dt_ref, h_scr):
    @pl.when(pl.program_id(2) == 0)
    def _():
        m = mod_ref[0]
        h = (x_ref[0] * (1.0 + m[1:2]) + m[0:1]).astype(BF16)
        h_scr[...] = h
        dt_ref[0] = _dot(h, wdt_ref[...])

    p_ref[0] = _dot(h_scr[...], w_ref[...]).astype(BF16)


def _inproj(x3, mod, w_main, w_dt, tm):
    g, r, d = x3.shape
    tn = 1152
    return pl.pallas_call(
        _inproj_kernel,
        grid=(g, r // tm, P_COLS // tn),
        in_specs=[pl.BlockSpec((1, tm, d), lambda b, i, j: (b, i, 0)),
                  pl.BlockSpec((1, 8, d), lambda b, i, j: (b, 0, 0)),
                  pl.BlockSpec((d, tn), lambda b, i, j: (0, j)),
                  pl.BlockSpec((d, DT_COLS), lambda b, i, j: (0, 0))],
        out_specs=[pl.BlockSpec((1, tm, tn), lambda b, i, j: (b, i, j)),
                   pl.BlockSpec((1, tm, DT_COLS), lambda b, i, j: (b, i, 0))],
        out_shape=[jax.ShapeDtypeStruct((g, r, P_COLS), BF16),
                   jax.ShapeDtypeStruct((g, r, DT_COLS), F32)],
        scratch_shapes=[pltpu.VMEM((tm, d), BF16)],
        compiler_params=_cp("parallel", "parallel", "arbitrary"),
        name="inproj",
    )(x3, mod, w_main, w_dt)


GA_TK = 512
LOG2E = math.log2(math.e)


def _prep_kernel(swq_ref, gaq_ref, swk_ref, gak_ref, gav_ref, cos_ref, sa_ref, sb_ref, gq_ref, gk_ref,
                 oswq, ogaq, oswk, ogak, ogavt, *, rope, pad_q, vt_chunk):
    row = lax.broadcasted_iota(jnp.int32, (128, 128), 0) // HEAD_DIM
    col = lax.broadcasted_iota(jnp.int32, (128, 128), 1) // HEAD_DIM
    head_mean = jnp.where(row == col, 1.0 / HEAD_DIM, 0.0).astype(F32)
    first_half = lax.broadcasted_iota(jnp.int32, (1, 128), 1) < HEAD_DIM

    def rot(x):
        if not rope:
            return x
        return (x * cos_ref[...] + pltpu.roll(x, 128 - HEAD_DIM // 2, 1) * sa_ref[...]
                + pltpu.roll(x, HEAD_DIM // 2, 1) * sb_ref[...])

    def rms(x, g):
        ms = jnp.dot(x * x, head_mean, precision=HI, preferred_element_type=F32)
        return x * lax.rsqrt(ms + RMS_EPS) * g

    for c in range(4):
        sl = slice(c * 128, (c + 1) * 128)
        oswq[0, :, sl] = (rot(swq_ref[0, :, sl].astype(F32)) * ATTN_SCALE).astype(BF16)
        gq_c = rot(rms(gaq_ref[0, :, sl].astype(F32), gq_ref[...]))
        if pad_q:
            gq_c = gq_c * (ATTN_SCALE * LOG2E)
            swapped = pltpu.roll(gq_c, HEAD_DIM, 1)
            keep = first_half if c < 2 else jnp.logical_not(first_half)
            even, odd = (gq_c, swapped) if c < 2 else (swapped, gq_c)
            ogaq[0, :, 2 * c * 128:(2 * c + 1) * 128] = jnp.where(keep, even, 0.0).astype(BF16)
            ogaq[0, :, (2 * c + 1) * 128:(2 * c + 2) * 128] = jnp.where(keep, odd, 0.0).astype(BF16)
        else:
            ogaq[0, :, sl] = (gq_c * ATTN_SCALE).astype(BF16)
    oswk[0] = rot(swk_ref[0].astype(F32)).astype(BF16)
    ogak[0] = rot(rms(gak_ref[0].astype(F32), gk_ref[...])).astype(BF16)
    for cc in range(gav_ref.shape[1] // vt_chunk):
        ogavt[0, cc] = gav_ref[0, cc * vt_chunk:(cc + 1) * vt_chunk, :].astype(F32).T.astype(BF16)


def _prep(p, tables, gq, gk, tm, rope, pad_q, vt_chunk):
    g, r, _ = p.shape
    cosf, sin_a, sin_b = tables
    tab = pl.BlockSpec((tm, 128), lambda b, i: (i, 0))
    vec = pl.BlockSpec((1, 128), lambda b, i: (0, 0))
    qw = 1024 if pad_q else 512
    n_vt = tm // vt_chunk
    return pl.pallas_call(
        functools.partial(_prep_kernel, rope=rope, pad_q=pad_q, vt_chunk=vt_chunk),
        grid=(g, r // tm),
        in_specs=[pl.BlockSpec((1, tm, 512), lambda b, i: (b, i, C512_SWQ)),
                  pl.BlockSpec((1, tm, 512), lambda b, i: (b, i, C512_GAQ)),
                  pl.BlockSpec((1, tm, 128), lambda b, i: (b, i, C128_SWK)),
                  pl.BlockSpec((1, tm, 128), lambda b, i: (b, i, C128_GAK)),
                  pl.BlockSpec((1, tm, 128), lambda b, i: (b, i, C128_GAV)),
                  tab, tab, tab, vec, vec],
        out_specs=[pl.BlockSpec((1, tm, 512), lambda b, i: (b, i, 0)),
                   pl.BlockSpec((1, tm, qw), lambda b, i: (b, i, 0)),
                   pl.BlockSpec((1, tm, 128), lambda b, i: (b, i, 0)),
                   pl.BlockSpec((1, tm, 128), lambda b, i: (b, i, 0)),
                   pl.BlockSpec((1, n_vt, 128, vt_chunk), lambda b, i: (b, i, 0, 0))],
        out_shape=[jax.ShapeDtypeStruct((g, r, 512), BF16), jax.ShapeDtypeStruct((g, r, qw), BF16),
                   jax.ShapeDtypeStruct((g, r, 128), BF16), jax.ShapeDtypeStruct((g, r, 128), BF16),
                   jax.ShapeDtypeStruct((g, r // vt_chunk, 128, vt_chunk), BF16)],
        compiler_params=_cp("parallel", "parallel"),
        name="qk_prep",
    )(p, p, p, p, p, cosf, sin_a, sin_b, gq, gk)


def _rope_tables(n_tok):
    t = jnp.arange(n_tok, dtype=jnp.int32)
    row = (t // GRID_W).astype(F32)
    col = (t % GRID_W).astype(F32)
    n_freq = HEAD_DIM // 4
    inv_freq = ROPE_THETA ** (-jnp.arange(n_freq, dtype=F32) / n_freq)
    ang = jnp.concatenate([row[:, None] * inv_freq, col[:, None] * inv_freq], axis=-1)
    cos, sin = jnp.cos(ang), jnp.sin(ang)
    zero = jnp.zeros_like(sin)
    cosf = jnp.tile(jnp.concatenate([cos, cos], -1), (1, 2))
    sin_a = jnp.tile(jnp.concatenate([-sin, zero], -1), (1, 2))
    sin_b = jnp.tile(jnp.concatenate([zero, sin], -1), (1, 2))
    return cosf, sin_a, sin_b


def _conv_kernel(prev_ref, cur_ref, next_ref, w_ref, b_ref, o_ref, *, tc):
    i = pl.program_id(1)
    n = pl.num_programs(1)
    prev = jnp.where(i > 0, prev_ref[0].astype(F32), 0.0)
    nxt = jnp.where(i < n - 1, next_ref[0].astype(F32), 0.0)
    ext = jnp.concatenate([prev, cur_ref[0].astype(F32), nxt], axis=0)
    rows = tc + 32
    acc = b_ref[...] + jnp.zeros((tc, ext.shape[1]), F32)
    for j in range(SSM_CONV):
        shift = (SSM_CONV // 2 - j) % rows
        sh = ext if shift == 0 else pltpu.roll(ext, shift, 0)
        acc = acc + sh[16:16 + tc] * w_ref[j:j + 1, :]
    o_ref[0] = _silu(acc).astype(BF16)


def _conv(p, w8, b, tc):
    g, r, _ = p.shape
    nb16 = r // 16
    k = tc // 16
    return pl.pallas_call(
        functools.partial(_conv_kernel, tc=tc),
        grid=(g, r // tc),
        in_specs=[pl.BlockSpec((1, 16, 1024), lambda b_, i: (b_, jnp.maximum(i * k - 1, 0), C1024_XBC)),
                  pl.BlockSpec((1, tc, 1024), lambda b_, i: (b_, i, C1024_XBC)),
                  pl.BlockSpec((1, 16, 1024), lambda b_, i: (b_, jnp.minimum((i + 1) * k, nb16 - 1), C1024_XBC)),
                  pl.BlockSpec((8, 1024), lambda b_, i: (0, 0)),
                  pl.BlockSpec((1, 1024), lambda b_, i: (0, 0))],
        out_specs=pl.BlockSpec((1, tc, 1024), lambda b_, i: (b_, i, 0)),
        out_shape=jax.ShapeDtypeStruct((g, r, 1024), BF16),
        compiler_params=_cp("parallel", "parallel"),
        name="dwconv_silu",
    )(p, p, p, w8, b)


def _softmax_out(units):
    probs = []
    for parts, sink in units:
        m = parts[0][0].max(axis=-1, keepdims=True)
        for s, _ in parts[1:]:
            m = jnp.maximum(m, s.max(axis=-1, keepdims=True))
        if sink is not None:
            m = jnp.maximum(m, sink)
        l = jnp.exp(sink - m) if sink is not None else 0.0
        es = []
        for s, _ in parts:
            e = jnp.exp(s - m)
            l = l + e.sum(axis=-1, keepdims=True)
            es.append(e.astype(BF16))
        probs.append((es, l))
    outs = []
    for (parts, _), (es, l) in zip(units, probs):
        acc = _dot(es[0], parts[0][1])
        for e, (_, v) in zip(es[1:], parts[1:]):
            acc = acc + _dot(e, v)
        outs.append(acc / l)
    return outs


NA_ROWS_PER_STEP = 4


def _na_kernel(q_ref, k_ref, v_ref, kc_ref, vc_ref, bias_ref, o_ref, *, n_rows):
    blk = pl.program_id(2)
    kc = kc_ref[0]
    vc = vc_ref[0]

    units = []
    for rr in range(NA_ROWS_PER_STEP):
        r = blk * NA_ROWS_PER_STEP + rr
        start = jnp.clip(r - NA_KH // 2, 0, n_rows - NA_KH)
        off = start - r + NA_KH - 1
        q = q_ref[0, rr * GRID_W:(rr + 1) * GRID_W, :]
        k0 = pl.multiple_of(start * GRID_W, GRID_W)
        kw = k_ref[0, pl.ds(k0, NA_KH * GRID_W), :]
        vw = v_ref[0, pl.ds(k0, NA_KH * GRID_W), :]
        for hh in range(2):
            sl = slice(hh * HEAD_DIM, (hh + 1) * HEAD_DIM)
            qh = q[:, sl] * ATTN_SCALE
            s_loc = _dot_nt(qh, kw[:, sl]) + bias_ref[off, hh]
            s_ctx = _dot_nt(qh, kc[:, sl])
            units.append(([(s_loc, vw[:, sl]), (s_ctx, vc[:, sl])], None))
    outs = _softmax_out(units)
    for rr in range(NA_ROWS_PER_STEP):
        o_ref[0, rr * GRID_W:(rr + 1) * GRID_W, :] = jnp.concatenate(outs[2 * rr:2 * rr + 2], axis=-1).astype(BF16)


def _na_bias_table(rpb):
    o = jnp.arange(NA_KH)
    kr = jnp.arange(NA_KH)
    dr = o[:, None] + kr[None, :]
    c = jnp.arange(GRID_W)
    kcol = jnp.arange(GRID_W)
    cs = jnp.clip(c - NA_KW // 2, 0, GRID_W - NA_KW)
    dc = kcol[None, :] - c[:, None] + NA_KW - 1
    valid = (kcol[None, :] >= cs[:, None]) & (kcol[None, :] < cs[:, None] + NA_KW)
    tbl = rpb[:, dr][:, :, :, jnp.clip(dc, 0, 2 * NA_KW - 2)]
    tbl = jnp.where(valid[None, None, None], tbl.astype(F32), NEG_INF)
    return tbl.transpose(1, 0, 3, 2, 4).reshape(NA_KH, NA_HEADS, GRID_W, NA_KH * GRID_W)


def _na_attention(p, pc, bias):
    b, s, _ = p.shape
    lc = pc.shape[1]
    n_rows = s // GRID_W
    tq = NA_ROWS_PER_STEP * GRID_W
    return pl.pallas_call(
        functools.partial(_na_kernel, n_rows=n_rows),
        grid=(b, NA_HEADS // 2, s // tq),
        in_specs=[pl.BlockSpec((1, tq, 128), lambda b_, h, i: (b_, i, 4 * C512_NAQ + h)),
                  pl.BlockSpec((1, s, 128), lambda b_, h, i: (b_, 0, 4 * C512_NAK + h)),
                  pl.BlockSpec((1, s, 128), lambda b_, h, i: (b_, 0, 4 * C512_NAV + h)),
                  pl.BlockSpec((1, lc, 128), lambda b_, h, i: (b_, 0, 4 * C512_NAK + h)),
                  pl.BlockSpec((1, lc, 128), lambda b_, h, i: (b_, 0, 4 * C512_NAV + h)),
                  pl.BlockSpec((NA_KH, 2, GRID_W, NA_KH * GRID_W), lambda b_, h, i: (0, h, 0, 0))],
        out_specs=pl.BlockSpec((1, tq, 128), lambda b_, h, i: (b_, i, h)),
        out_shape=jax.ShapeDtypeStruct((b, s, 512), BF16),
        compiler_params=_cp("parallel", "parallel", "arbitrary"),
        name="na_attention",
    )(p, p, p, pc, pc, bias)


SWA_TQ = 256
SWA_BAND = SWA_TQ + 2 * SWA_WINDOW


def _swa_kernel(q_ref, k_ref, v_ref, kc_ref, vc_ref, sink_ref, o_ref, *, n_tok):
    n = pl.program_id(1)
    start = pl.multiple_of(jnp.clip(n * SWA_TQ - SWA_WINDOW, 0, n_tok - SWA_BAND), SWA_WINDOW)
    kb = k_ref[0, pl.ds(start, SWA_BAND), :]
    vb = v_ref[0, pl.ds(start, SWA_BAND), :]
    kc = kc_ref[0]
    vc = vc_ref[0]
    qpos = n * SWA_TQ + lax.broadcasted_iota(jnp.int32, (SWA_TQ, SWA_BAND), 0)
    kpos = start + lax.broadcasted_iota(jnp.int32, (SWA_TQ, SWA_BAND), 1)
    in_win = jnp.abs(kpos - qpos) <= SWA_WINDOW
    units = []
    for h in range(8):
        g = h // 4
        gs = slice(g * HEAD_DIM, (g + 1) * HEAD_DIM)
        qh = q_ref[0, :, h * HEAD_DIM:(h + 1) * HEAD_DIM]
        s_loc = jnp.where(in_win, _dot_nt(qh, kb[:, gs]), NEG_INF)
        s_ctx = _dot_nt(qh, kc[:, gs])
        units.append(([(s_loc, vb[:, gs]), (s_ctx, vc[:, gs])], sink_ref[h:h + 1, 0:1]))
    o_ref[0] = jnp.concatenate(_softmax_out(units), axis=-1).astype(BF16)


def _swa_attention(q, k, p, kc, pc, sink_tab):
    b, s, _ = q.shape
    lc = kc.shape[1]
    return pl.pallas_call(
        functools.partial(_swa_kernel, n_tok=s),
        grid=(b, s // SWA_TQ),
        in_specs=[pl.BlockSpec((1, SWA_TQ, 512), lambda b_, i: (b_, i, 0)),
                  pl.BlockSpec((1, s, 128), lambda b_, i: (b_, 0, 0)),
                  pl.BlockSpec((1, s, 128), lambda b_, i: (b_, 0, C128_SWV)),
                  pl.BlockSpec((1, lc, 128), lambda b_, i: (b_, 0, 0)),
                  pl.BlockSpec((1, lc, 128), lambda b_, i: (b_, 0, C128_SWV)),
                  pl.BlockSpec((8, 128), lambda b_, i: (0, 0))],
        out_specs=pl.BlockSpec((1, SWA_TQ, 512), lambda b_, i: (b_, i, 0)),
        out_shape=jax.ShapeDtypeStruct((b, s, 512), BF16),
        compiler_params=_cp("parallel", "arbitrary"),
        name="swa_attention",
    )(q, k, p, kc, pc, sink_tab)


GA_TQ = 256


def _ga_kernel(q_ref, k_ref, vt_ref, kc_ref, vct_ref, o_ref, acc_scr, *, n_chunks):
    acc_scr[...] = jnp.zeros_like(acc_scr)

    def step(carry, kk, vt):
        m_all, l_all = carry

        def scores(h):
            return _dot_nt(kk, q_ref[0, :, h * 128:(h + 1) * 128])

        m_rows, l_rows = [], []
        sts = [scores(h) for h in range(8)]
        for h in range(8):
            st = sts[h]
            g = h // 4
            m_old = m_all[h:h + 1, :]
            m_new = jnp.maximum(m_old, st.max(axis=0, keepdims=True))
            a = jnp.exp2(m_old - m_new)
            e = jnp.exp2(st - m_new)
            l_rows.append(a * l_all[h:h + 1, :] + e.sum(axis=0, keepdims=True))
            m_rows.append(m_new)
            rows = slice(h * HEAD_DIM, (h + 1) * HEAD_DIM)
            acc_scr[rows, :] = a * acc_scr[rows, :] + _dot(vt[g * HEAD_DIM:(g + 1) * HEAD_DIM, :], e.astype(BF16))
        return jnp.concatenate(m_rows, axis=0), jnp.concatenate(l_rows, axis=0)

    def chunk(c, carry):
        k0 = pl.multiple_of(c * GA_TK, GA_TK)
        return step(carry, k_ref[0, pl.ds(k0, GA_TK), :], vt_ref[0, c])

    init = (jnp.full((8, GA_TQ), NEG_INF, F32), jnp.zeros((8, GA_TQ), F32))
    carry = lax.fori_loop(0, n_chunks, chunk, init)
    _, l_fin = step(carry, kc_ref[0], vct_ref[0, 0])
    inv = 1.0 / l_fin
    out_t = jnp.concatenate([acc_scr[h * HEAD_DIM:(h + 1) * HEAD_DIM, :] * inv[h:h + 1, :] for h in range(8)], axis=0)
    o_ref[0] = out_t.T.astype(BF16)


def _ga_attention(q_pad, k, vt, kc, vct):
    b, s, _ = k.shape
    lc = kc.shape[1]
    n_chunks = s // GA_TK
    return pl.pallas_call(
        functools.partial(_ga_kernel, n_chunks=n_chunks),
        grid=(b, s // GA_TQ),
        in_specs=[pl.BlockSpec((1, GA_TQ, 1024), lambda b_, i: (b_, i, 0)),
                  pl.BlockSpec((1, s, 128), lambda b_, i: (b_, 0, 0)),
                  pl.BlockSpec((1, n_chunks, 128, GA_TK), lambda b_, i: (b_, 0, 0, 0)),
                  pl.BlockSpec((1, lc, 128), lambda b_, i: (b_, 0, 0)),
                  pl.BlockSpec((1, 1, 128, lc), lambda b_, i: (b_, 0, 0, 0))],
        out_specs=pl.BlockSpec((1, GA_TQ, 512), lambda b_, i: (b_, i, 0)),
        out_shape=jax.ShapeDtypeStruct((b, s, 512), BF16),
        scratch_shapes=[pltpu.VMEM((8 * HEAD_DIM, GA_TQ), F32)],
        compiler_params=_cp("parallel", "arbitrary"),
        name="ga_attention",
    )(q_pad, k, vt, kc, vct)


def _ctx_attn_kernel(q_ref, k_ref, v_ref, sink_ref, o_ref, *, n_kv, q_scale, use_sink):
    rep = 8 // n_kv
    units = []
    for h in range(8):
        g = h // rep
        gs = slice(g * HEAD_DIM, (g + 1) * HEAD_DIM)
        qh = q_ref[0, :, h * HEAD_DIM:(h + 1) * HEAD_DIM]
        if q_scale:
            qh = qh * ATTN_SCALE
        s = _dot_nt(qh, k_ref[0, :, gs])
        units.append(([(s, v_ref[0, :, gs])], sink_ref[h:h + 1, 0:1] if use_sink else None))
    o_ref[0] = jnp.concatenate(_softmax_out(units), axis=-1).astype(BF16)


def _ctx_attention(q, qcol, k, kcol, v, vcol, n_kv, sink_tab, q_scale, use_sink):
    b, lc, _ = q.shape
    kvw = n_kv * HEAD_DIM
    return pl.pallas_call(
        functools.partial(_ctx_attn_kernel, n_kv=n_kv, q_scale=q_scale, use_sink=use_sink),
        grid=(b,),
        in_specs=[pl.BlockSpec((1, lc, 512), lambda b_: (b_, 0, qcol)),
                  pl.BlockSpec((1, lc, kvw), lambda b_: (b_, 0, kcol)),
                  pl.BlockSpec((1, lc, kvw), lambda b_: (b_, 0, vcol)),
                  pl.BlockSpec((8, 128), lambda b_: (0, 0))],
        out_specs=pl.BlockSpec((1, lc, 512), lambda b_: (b_, 0, 0)),
        out_shape=jax.ShapeDtypeStruct((b, lc, 512), BF16),
        compiler_params=_cp("parallel"),
        name="ctx_attention",
    )(q, k, v, sink_tab)


def _ssd_kernel(u_ref, dt_ref, dtb_ref, a_ref, h0_ref, y_ref, hT_ref, st, *, q_len):
    d = pl.program_id(1)
    c = pl.program_id(2)

    @pl.when(c == 0)
    def _():
        st[...] = h0_ref[0, 0]

    u = u_ref[0]
    dt = _softplus(dt_ref[0] + dtb_ref[...])
    da = dt * a_ref[...]
    row = lax.broadcasted_iota(jnp.int32, (q_len, q_len), 0)
    col = lax.broadcasted_iota(jnp.int32, (q_len, q_len), 1)
    causal = (row - col) * (1 - 2 * d) >= 0
    acum = jnp.dot(causal.astype(F32), da, precision=HI, preferred_element_type=F32)
    acum_t = acum.T
    total = jnp.sum(da, axis=0, keepdims=True)
    ys = []
    for g in range(2):
        bg = u[:, SSM_INNER + g * SSM_STATE:SSM_INNER + (g + 1) * SSM_STATE]
        cg = u[:, SSM_INNER + (2 + g) * SSM_STATE:SSM_INNER + (3 + g) * SSM_STATE]
        cb = _dot_nt(cg, bg)
        bg_t = bg.astype(F32).T.astype(BF16)
        s_g = st[g]
        y_off = _dot(cg, s_g.astype(BF16))
        new_cols = []
        for hh in range(4):
            h = g * 4 + hh
            hs = slice(hh * HEAD_DIM, (hh + 1) * HEAD_DIM)
            ac = acum[:, h:h + 1]
            seg = jnp.exp(jnp.where(causal, ac - acum_t[h:h + 1, :], NEG_INF))
            xdt = u[:, h * HEAD_DIM:(h + 1) * HEAD_DIM].astype(F32) * dt[:, h:h + 1]
            y_h = _dot((cb * seg).astype(BF16), xdt.astype(BF16)) + y_off[:, hs] * jnp.exp(ac)
            ys.append(y_h)
            tot = total[:, h:h + 1]
            xdec = (xdt * jnp.exp(tot - ac)).astype(BF16)
            new_cols.append(s_g[:, hs] * jnp.exp(tot) + _dot(bg_t, xdec))
        st[g] = jnp.concatenate(new_cols, axis=-1)
    y_ref[0, 0] = jnp.concatenate(ys, axis=-1)

    @pl.when(c == pl.num_programs(2) - 1)
    def _():
        hT_ref[0, 0] = st[...]


def _ssd(u, dt_raw, dtb, a_full, h0, q_len):
    b, l, _ = u.shape
    n_c = l // q_len

    def cidx(d, c):
        return c + d * (n_c - 1 - 2 * c)

    return pl.pallas_call(
        functools.partial(_ssd_kernel, q_len=q_len),
        grid=(b, 2, n_c),
        in_specs=[pl.BlockSpec((1, q_len, 1024), lambda b_, d, c: (b_, cidx(d, c), 0)),
                  pl.BlockSpec((1, q_len, 128), lambda b_, d, c: (b_, cidx(d, c), d)),
                  pl.BlockSpec((1, 128), lambda b_, d, c: (0, d)),
                  pl.BlockSpec((1, 128), lambda b_, d, c: (0, d)),
                  pl.BlockSpec((1, 1, 2, SSM_STATE, 256), lambda b_, d, c: (b_, d, 0, 0, 0))],
        out_specs=[pl.BlockSpec((1, 1, q_len, 512), lambda b_, d, c: (d, b_, cidx(d, c), 0)),
                   pl.BlockSpec((1, 1, 2, SSM_STATE, 256), lambda b_, d, c: (b_, d, 0, 0, 0))],
        out_shape=[jax.ShapeDtypeStruct((2, b, l, 512), F32),
                   jax.ShapeDtypeStruct((b, 2, 2, SSM_STATE, 256), F32)],
        scratch_shapes=[pltpu.VMEM((2, SSM_STATE, 256), F32)],
        compiler_params=_cp("parallel", "arbitrary", "arbitrary"),
        name="ssd_scan",
    )(u, dt_raw, dtb, a_full, h0)


def _ssd_finish_kernel(yf_ref, yb_ref, xs_ref, z_ref, dsk_ref, g_ref, o_ref):
    y = yf_ref[0, 0] + yb_ref[0, 0] + dsk_ref[...] * xs_ref[0].astype(F32)
    y = y * _silu(z_ref[0].astype(F32))
    ms = jnp.mean(y * y, axis=-1, keepdims=True)
    o_ref[0] = (y * lax.rsqrt(ms + RMS_EPS) * g_ref[...]).astype(BF16)


def _ssd_finish(y, u, p, dsk, g, tm):
    _, b, l, _ = y.shape
    vec = pl.BlockSpec((1, 512), lambda b_, i: (0, 0))
    return pl.pallas_call(
        _ssd_finish_kernel,
        grid=(b, l // tm),
        in_specs=[pl.BlockSpec((1, 1, tm, 512), lambda b_, i: (0, b_, i, 0)),
                  pl.BlockSpec((1, 1, tm, 512), lambda b_, i: (1, b_, i, 0)),
                  pl.BlockSpec((1, tm, 512), lambda b_, i: (b_, i, 0)),
                  pl.BlockSpec((1, tm, 512), lambda b_, i: (b_, i, C512_Z)),
                  vec, vec],
        out_specs=pl.BlockSpec((1, tm, 512), lambda b_, i: (b_, i, 0)),
        out_shape=jax.ShapeDtypeStruct((b, l, 512), BF16),
        compiler_params=_cp("parallel", "parallel"),
        name="ssd_finish",
    )(y, y, u, p, dsk, g)


def _merge_kernel(*refs, alpha, moe):
    if moe:
        (x_ref, mod_ref, oa, ob, oc, od, wg_ref, bg_ref, wb_ref, wo_ref, lng_ref, lnb_ref, wr_ref, br_ref,
         x1_ref, h2_ref, lg_ref, h_scr, acc_scr) = refs
    else:
        (x_ref, mod_ref, oa, ob, oc, od, wg_ref, bg_ref, wb_ref, wo_ref, lng_ref, lnb_ref,
         x1_ref, h_scr, acc_scr) = refs
    i = pl.program_id(2)
    m = mod_ref[0]

    @pl.when(i == 0)
    def _():
        h_scr[...] = (x_ref[0] * (1.0 + m[1:2]) + m[0:1]).astype(BF16)
        acc_scr[...] = jnp.zeros_like(acc_scr)

    for k, br in enumerate((oa, ob, oc, od)):
        @pl.when(i == k)
        def _(br=br):
            gate = _sigmoid(_dot(h_scr[...], wg_ref[0]) + bg_ref[0])
            acc_scr[...] += gate * _dot(br[0], wb_ref[0])

    @pl.when(i == N_BRANCH - 1)
    def _():
        o = _dot(acc_scr[...].astype(BF16), wo_ref[...])
        x1 = _layer_norm(alpha * x_ref[0] + m[2:3] * o, lng_ref[...], lnb_ref[...])
        x1_ref[0] = x1
        if moe:
            h2 = x1 * (1.0 + m[4:5]) + m[3:4]
            h2_ref[0] = h2
            lg_ref[0] = jnp.dot(h2, wr_ref[...], precision=HI, preferred_element_type=F32) + br_ref[...]


def _merge(x3, mod, branches, wg, bg, wb, wo, lng, lnb, alpha, tm, router=None):
    g, r, d = x3.shape
    moe = router is not None
    tile = lambda w: pl.BlockSpec((1, tm, w), lambda b, t, i: (b, t, 0))
    vec = pl.BlockSpec((1, d), lambda b, t, i: (0, 0))
    in_specs = [tile(d), pl.BlockSpec((1, 8, d), lambda b, t, i: (b, 0, 0)),
                tile(BRANCH_W), tile(BRANCH_W), tile(BRANCH_W), tile(BRANCH_W),
                pl.BlockSpec((1, d, d), lambda b, t, i: (i, 0, 0)),
                pl.BlockSpec((1, 1, d), lambda b, t, i: (i, 0, 0)),
                pl.BlockSpec((1, BRANCH_W, d), lambda b, t, i: (i, 0, 0)),
                pl.BlockSpec((d, d), lambda b, t, i: (0, 0)), vec, vec]
    args = [x3, mod, *branches, wg, bg, wb, wo, lng, lnb]
    out_specs = [tile(d)]
    out_shape = [jax.ShapeDtypeStruct((g, r, d), F32)]
    if moe:
        in_specs += [pl.BlockSpec((d, 128), lambda b, t, i: (0, 0)), pl.BlockSpec((1, 128), lambda b, t, i: (0, 0))]
        args += list(router)
        out_specs += [tile(d), tile(128)]
        out_shape += [jax.ShapeDtypeStruct((g, r, d), F32), jax.ShapeDtypeStruct((g, r, 128), F32)]
    return pl.pallas_call(
        functools.partial(_merge_kernel, alpha=alpha, moe=moe),
        grid=(g, r // tm, N_BRANCH),
        in_specs=in_specs, out_specs=out_specs, out_shape=out_shape,
        scratch_shapes=[pltpu.VMEM((tm, d), BF16), pltpu.VMEM((tm, d), F32)],
        compiler_params=_cp("parallel", "parallel", "arbitrary"),
        name="merge_ln1",
    )(*args)


FFN_CHUNK = D_FF // 2


def _ffn_kernel(x_ref, mod_ref, wg_ref, wu_ref, wd_ref, lng_ref, lnb_ref, o_ref, h_scr, acc_scr, *, alpha):
    j = pl.program_id(2)
    m = mod_ref[0]

    @pl.when(j == 0)
    def _():
        h_scr[...] = (x_ref[0] * (1.0 + m[4:5]) + m[3:4]).astype(BF16)
        acc_scr[...] = jnp.zeros_like(acc_scr)

    h = h_scr[...]
    a = _silu(_dot(h, wg_ref[...])) * _dot(h, wu_ref[...])
    acc_scr[...] += _dot(a.astype(BF16), wd_ref[...])

    @pl.when(j == pl.num_programs(2) - 1)
    def _():
        o_ref[0] = _layer_norm(alpha * x_ref[0] + m[5:6] * acc_scr[...], lng_ref[...], lnb_ref[...])


def _ffn(x3, mod, w_up, w_down, lng, lnb, alpha, tm):
    g, r, d = x3.shape
    n_j = D_FF // FFN_CHUNK
    vec = pl.BlockSpec((1, d), lambda b, t, j: (0, 0))
    return pl.pallas_call(
        functools.partial(_ffn_kernel, alpha=alpha),
        grid=(g, r // tm, n_j),
        in_specs=[pl.BlockSpec((1, tm, d), lambda b, t, j: (b, t, 0)),
                  pl.BlockSpec((1, 8, d), lambda b, t, j: (b, 0, 0)),
                  pl.BlockSpec((d, FFN_CHUNK), lambda b, t, j: (0, j)),
                  pl.BlockSpec((d, FFN_CHUNK), lambda b, t, j: (0, n_j + j)),
                  pl.BlockSpec((FFN_CHUNK, d), lambda b, t, j: (j, 0)), vec, vec],
        out_specs=pl.BlockSpec((1, tm, d), lambda b, t, j: (b, t, 0)),
        out_shape=jax.ShapeDtypeStruct((g, r, d), F32),
        scratch_shapes=[pltpu.VMEM((tm, d), BF16), pltpu.VMEM((tm, d), F32)],
        compiler_params=_cp("parallel", "parallel", "arbitrary"),
        name="ffn_ln2",
    )(x3, mod, w_up, w_up, w_down, lng, lnb)


ROUTE_TM = 256


def _route_kernel(lg_ref, info_ref, cnt_ref, carry):
    i = pl.program_id(0)

    @pl.when(i == 0)
    def _():
        carry[...] = jnp.zeros_like(carry)

    lg = lg_ref[...]
    tm = lg.shape[0]
    lane = lax.broadcasted_iota(jnp.int32, lg.shape, 1).astype(F32)
    m1 = lg.max(axis=-1, keepdims=True)
    i1 = jnp.where(lg == m1, lane, 128.0).min(axis=-1, keepdims=True)
    lg2 = jnp.where(lane == i1, -jnp.inf, lg)
    m2 = lg2.max(axis=-1, keepdims=True)
    i2 = jnp.where(lg2 == m2, lane, 128.0).min(axis=-1, keepdims=True)
    e = jnp.exp(m2 - m1)
    w1 = 1.0 / (1.0 + e)
    w2 = e / (1.0 + e)
    sel = jnp.where((lane == i1) | (lane == i2), 1.0, 0.0).astype(F32)
    row = lax.broadcasted_iota(jnp.int32, (tm, tm), 0)
    col = lax.broadcasted_iota(jnp.int32, (tm, tm), 1)
    before = jnp.where(row > col, 1.0, 0.0).astype(BF16)
    excl = _dot(before, sel.astype(BF16)) + carry[0:1, :]
    r1 = jnp.where(lane == i1, excl, 0.0).sum(axis=-1, keepdims=True)
    r2 = jnp.where(lane == i2, excl, 0.0).sum(axis=-1, keepdims=True)
    carry[...] = carry[...] + sel.sum(axis=0, keepdims=True)
    info = jnp.where(lane == 0, i1, 0.0)
    info = jnp.where(lane == 1, i2, info)
    info = jnp.where(lane == 2, r1, info)
    info = jnp.where(lane == 3, r2, info)
    info = jnp.where(lane == 4, w1, info)
    info = jnp.where(lane == 5, w2, info)
    info_ref[...] = info
    cnt_ref[...] = carry[...]


def _route(logits):
    t = logits.shape[0]
    return pl.pallas_call(
        _route_kernel,
        grid=(t // ROUTE_TM,),
        in_specs=[pl.BlockSpec((ROUTE_TM, 128), lambda i: (i, 0))],
        out_specs=[pl.BlockSpec((ROUTE_TM, 128), lambda i: (i, 0)), pl.BlockSpec((8, 128), lambda i: (0, 0))],
        out_shape=[jax.ShapeDtypeStruct((t, 128), F32), jax.ShapeDtypeStruct((8, 128), F32)],
        scratch_shapes=[pltpu.VMEM((8, 128), F32)],
        compiler_params=_cp("arbitrary"),
        name="moe_route",
    )(logits)


SCATTER_TM = 256


def _scatter_kernel(d_ref, tok_ref, zero_ref, out_ref, sem):
    del zero_ref

    def copy(t, k):
        return pltpu.make_async_copy(tok_ref.at[pl.ds(t, 1)], out_ref.at[pl.ds(d_ref[0, 0, 2 * t + k], 1)], sem)

    def issue(t, carry):
        copy(t, 0).start()
        copy(t, 1).start()
        return carry

    def drain(t, carry):
        copy(t, 0).wait()
        copy(t, 1).wait()
        return carry

    lax.fori_loop(0, SCATTER_TM, issue, 0)
    lax.fori_loop(0, SCATTER_TM, drain, 0)


def _scatter_rows(dest, tok, n_pad):
    t, d = tok.shape
    n_t = t // SCATTER_TM
    return pl.pallas_call(
        _scatter_kernel,
        grid=(n_t,),
        in_specs=[pl.BlockSpec((1, 1, 2 * SCATTER_TM), lambda i: (i, 0, 0), memory_space=pltpu.SMEM),
                  pl.BlockSpec((SCATTER_TM, d), lambda i: (i, 0)),
                  pl.BlockSpec(memory_space=pl.ANY)],
        out_specs=pl.BlockSpec(memory_space=pl.ANY),
        out_shape=jax.ShapeDtypeStruct((n_pad, d), F32),
        scratch_shapes=[pltpu.SemaphoreType.DMA(())],
        input_output_aliases={2: 0},
        compiler_params=_cp("arbitrary"),
        name="moe_scatter",
    )(dest.reshape(n_t, 1, 2 * SCATTER_TM), tok, jnp.zeros((n_pad, d), F32))


EXP_CHUNK = D_FF_EXPERT // 4


def _expert_kernel(be_ref, nv_ref, x_ref, wg_ref, wu_ref, wd_ref, y_ref):
    del be_ref
    b = pl.program_id(0)
    j = pl.program_id(1)

    @pl.when(j == 0)
    def _():
        y_ref[...] = jnp.zeros_like(y_ref)

    @pl.when(b < nv_ref[0])
    def _():
        h = x_ref[...].astype(BF16)
        a = _silu(_dot(h, wg_ref[0])) * _dot(h, wu_ref[0])
        y_ref[...] += _dot(a.astype(BF16), wd_ref[0])


def _experts(x_pad, blk_expert, n_valid, w_up, w_down):
    n_pad, d = x_pad.shape
    n_blk = n_pad // MOE_BLOCK
    n_j = D_FF_EXPERT // EXP_CHUNK

    def jj(b, j, nv):
        return jnp.where(b < nv[0], j, n_j - 1)

    grid_spec = pltpu.PrefetchScalarGridSpec(
        num_scalar_prefetch=2,
        grid=(n_blk, n_j),
        in_specs=[pl.BlockSpec((MOE_BLOCK, d), lambda b, j, be, nv: (b, 0)),
                  pl.BlockSpec((1, d, EXP_CHUNK), lambda b, j, be, nv: (be[b], 0, jj(b, j, nv))),
                  pl.BlockSpec((1, d, EXP_CHUNK), lambda b, j, be, nv: (be[b], 0, n_j + jj(b, j, nv))),
                  pl.BlockSpec((1, EXP_CHUNK, d), lambda b, j, be, nv: (be[b], jj(b, j, nv), 0))],
        out_specs=pl.BlockSpec((MOE_BLOCK, d), lambda b, j, be, nv: (b, 0)),
    )
    return pl.pallas_call(
        _expert_kernel,
        grid_spec=grid_spec,
        out_shape=jax.ShapeDtypeStruct((n_pad, d), F32),
        compiler_params=_cp("arbitrary", "arbitrary"),
        name="moe_experts",
    )(blk_expert, n_valid, x_pad, w_up, w_up, w_down)


COMBINE_TM = 256


def _combine_kernel(d_ref, info_ref, x_ref, mod_ref, lng_ref, lnb_ref, y_ref, o_ref, gbuf, sem, *, alpha):
    def copy(t, k):
        return pltpu.make_async_copy(y_ref.at[pl.ds(d_ref[0, 0, 2 * t + k], 1)], gbuf.at[k, pl.ds(t, 1)], sem)

    def issue(t, carry):
        copy(t, 0).start()
        copy(t, 1).start()
        return carry

    def drain(t, carry):
        copy(t, 0).wait()
        copy(t, 1).wait()
        return carry

    lax.fori_loop(0, COMBINE_TM, issue, 0)
    lax.fori_loop(0, COMBINE_TM, drain, 0)
    m = mod_ref[0]
    info = info_ref[0]
    f = info[:, 4:5] * gbuf[0] + info[:, 5:6] * gbuf[1]
    o_ref[0] = _layer_norm(alpha * x_ref[0] + m[5:6] * f, lng_ref[...], lnb_ref[...])


def _combine(dest, info, x3, mod, lng, lnb, y_pad, alpha):
    g, r, d = x3.shape
    n_t = r // COMBINE_TM
    vec = pl.BlockSpec((1, d), lambda b, i: (0, 0))
    return pl.pallas_call(
        functools.partial(_combine_kernel, alpha=alpha),
        grid=(g, n_t),
        in_specs=[pl.BlockSpec((1, 1, 2 * COMBINE_TM), lambda b, i: (b * n_t + i, 0, 0), memory_space=pltpu.SMEM),
                  pl.BlockSpec((1, COMBINE_TM, 128), lambda b, i: (b, i, 0)),
                  pl.BlockSpec((1, COMBINE_TM, d), lambda b, i: (b, i, 0)),
                  pl.BlockSpec((1, 8, d), lambda b, i: (b, 0, 0)), vec, vec,
                  pl.BlockSpec(memory_space=pl.ANY)],
        out_specs=pl.BlockSpec((1, COMBINE_TM, d), lambda b, i: (b, i, 0)),
        out_shape=jax.ShapeDtypeStruct((g, r, d), F32),
        scratch_shapes=[pltpu.VMEM((2, COMBINE_TM, d), F32), pltpu.SemaphoreType.DMA(())],
        compiler_params=_cp("arbitrary", "arbitrary"),
        name="moe_combine_ln2",
    )(dest.reshape(g * n_t, 1, 2 * COMBINE_TM), info.reshape(g, r, 128), x3, mod, lng, lnb, y_pad)


def _moe_plan(info, counts_row):
    n_tok = info.shape[0]
    e = info[:, 0:2].astype(jnp.int32)
    rank = info[:, 2:4].astype(jnp.int32)
    counts = counts_row[:N_EXPERTS].astype(jnp.int32)
    padded = (counts + MOE_BLOCK - 1) // MOE_BLOCK * MOE_BLOCK
    pad_end = jnp.cumsum(padded)
    pad_start = pad_end - padded
    dest = pad_start[e] + rank
    n_blk = -(-n_tok * 2 // MOE_BLOCK) + N_EXPERTS
    blk_start = jnp.arange(n_blk, dtype=jnp.int32) * MOE_BLOCK
    blk_expert = jnp.minimum(jnp.sum(pad_end[None, :] <= blk_start[:, None], axis=1), N_EXPERTS - 1).astype(jnp.int32)
    n_valid = (pad_end[-1:] // MOE_BLOCK).astype(jnp.int32)
    return dest, blk_expert, n_valid, n_blk * MOE_BLOCK


def _pad_rows(a, rows):
    return jnp.concatenate([a, jnp.zeros((rows - a.shape[0],) + a.shape[1:], a.dtype)], axis=0)


def kernel(x, c, ctx, c_ctx, w_mod, b_mod, w_in, w_gate, b_gate, na_rpb, swa_sink, qk_gain_q, qk_gain_k, conv_w, conv_b, dt_bias, a_log, d_skip, ssm_norm_g, w_branch, w_out, ln1_g, ln1_b, ln2_g, ln2_b, ffn_w_up, ffn_w_down, moe_w_router, moe_b_router, moe_w_up, moe_w_down):
    bsz, seq, d = x.shape
    lc = ctx.shape[1]
    depth = w_mod.shape[0]
    alpha = (2 * depth) ** 0.25
    n_ctx = bsz * lc
    assert bsz < 8 and seq % 1024 == 0 and lc % 256 == 0 and d == D_MODEL

    s_rows = _pad_rows(jnp.concatenate([c, c_ctx[None, :]], axis=0), 8)
    mods = _modulation(s_rows, w_mod, b_mod).reshape(depth, 8, 6, d)
    mods = jnp.concatenate([mods, jnp.zeros((depth, 8, 2, d), F32)], axis=2)

    offs = [0, 512, 1024, 1536, 2048, 2176, 2304, 2816, 2944, 3072, 3584, 4608, 4624]
    seg = lambda k: w_in[:, :, offs[k]:offs[k + 1]]
    w_main = jnp.concatenate([seg(0), seg(1), seg(2), seg(3), seg(6), seg(9), seg(10), seg(4), seg(5), seg(7), seg(8)],
                             axis=-1).astype(BF16)
    zpad = jnp.zeros((depth, d, 128 - SSM_HEADS), F32)
    w_dt = jnp.concatenate([seg(11)[:, :, :SSM_HEADS], zpad, seg(11)[:, :, SSM_HEADS:], zpad], axis=-1).astype(BF16)

    w_gate_b = w_gate.astype(BF16)
    w_branch_b = w_branch.astype(BF16)
    w_out_b = w_out.astype(BF16)
    ffn_up_b = ffn_w_up.astype(BF16)
    ffn_down_b = ffn_w_down.astype(BF16)
    moe_up_b = moe_w_up.astype(BF16)
    moe_down_b = moe_w_down.astype(BF16)

    rope_tabs = _rope_tables(seq)
    a_neg = -jnp.exp(a_log.astype(F32))
    pad8 = lambda v: jnp.concatenate([v, jnp.zeros(v.shape[:-1] + (128 - SSM_HEADS,), F32)], axis=-1)

    ctx2 = ctx.reshape(1, n_ctx, d)
    for l in range(depth):
        with_ctx = l < depth - 1
        mod_lat = mods[l, :bsz]
        mod_ctx = mods[l, bsz:bsz + 1]

        p, dt_raw = _inproj(x, mod_lat, w_main[l], w_dt[l], 1024)
        pc, dtc_raw = _inproj(ctx2, mod_ctx, w_main[l], w_dt[l], n_ctx)
        pc = pc.reshape(bsz, lc, P_COLS)
        dtc_raw = dtc_raw.reshape(bsz, lc, DT_COLS)

        gq = jnp.tile(qk_gain_q[l].astype(F32), 2)[None, :]
        gk = jnp.tile(qk_gain_k[l].astype(F32), 2)[None, :]
        swq, gaq_pad, swk, gak, gavt = _prep(p, rope_tabs, gq, gk, 1024, True, True, GA_TK)
        swq_c, gaq_c, swk_c, gak_c, gavt_c = _prep(pc, rope_tabs, gq, gk, lc, False, False, lc)

        sink_tab = jnp.broadcast_to(swa_sink[l].astype(F32)[:, None], (8, 128))
        o_a = _na_attention(p, pc, _na_bias_table(na_rpb[l]))
        o_b = _swa_attention(swq, swk, p, swk_c, pc, sink_tab)
        o_c = _ga_attention(gaq_pad, gak, gavt, gak_c, gavt_c)

        w8 = _pad_rows(conv_w[l].astype(F32), 8)
        cb = conv_b[l].astype(F32)[None, :]
        u = _conv(p, w8, cb, 512)
        uc = _conv(pc, w8, cb, lc)
        dtb = jnp.concatenate([pad8(dt_bias[l, 0].astype(F32)), pad8(dt_bias[l, 1].astype(F32))])[None, :]
        a_full = jnp.concatenate([pad8(a_neg[l, 0]), pad8(a_neg[l, 1])])[None, :]
        h_zero = jnp.zeros((bsz, 2, 2, SSM_STATE, 256), F32)
        yc, h_ctx = _ssd(uc, dtc_raw, dtb, a_full, h_zero, lc)
        y, _ = _ssd(u, dt_raw, dtb, a_full, h_ctx, 256)
        dsk = jnp.repeat(d_skip[l].astype(F32), HEAD_DIM)[None, :]
        ng = ssm_norm_g[l].astype(F32)[None, :]
        o_d = _ssd_finish(y, u, p, dsk, ng, 1024)

        is_moe = l % 2 == 1
        i = l // 2
        router = None
        if is_moe:
            wr = jnp.concatenate([moe_w_router[i].astype(F32), jnp.zeros((d, 128 - N_EXPERTS), F32)], axis=-1)
            br = jnp.concatenate([moe_b_router[i].astype(F32), jnp.full((128 - N_EXPERTS,), NEG_INF, F32)])[None, :]
            router = (wr, br)
        lng1, lnb1 = ln1_g[l].astype(F32)[None, :], ln1_b[l].astype(F32)[None, :]
        lng2, lnb2 = ln2_g[l].astype(F32)[None, :], ln2_b[l].astype(F32)[None, :]
        bg = b_gate[l].astype(F32)[:, None, :]
        res = _merge(x, mod_lat, (o_a, o_b, o_c, o_d), w_gate_b[l], bg, w_branch_b[l], w_out_b[l], lng1, lnb1,
                     alpha, 512, router)
        if with_ctx:
            o_ac = _ctx_attention(pc, C512_NAQ, pc, C512_NAK, pc, C512_NAV, 8, sink_tab, True, False)
            o_bc = _ctx_attention(swq_c, 0, swk_c, 0, pc, C128_SWV, 2, sink_tab, False, True)
            o_cc = _ctx_attention(gaq_c, 0, gak_c, 0, pc, C128_GAV, 2, sink_tab, False, False)
            o_dc = _ssd_finish(yc, uc, pc, dsk, ng, lc)
            to2 = lambda t: t.reshape(1, n_ctx, BRANCH_W)
            res_c = _merge(ctx2, mod_ctx, (to2(o_ac), to2(o_bc), to2(o_cc), to2(o_dc)), w_gate_b[l], bg,
                           w_branch_b[l], w_out_b[l], lng1, lnb1, alpha, min(512, n_ctx), router)

        if not is_moe:
            x = _ffn(res[0], mod_lat, ffn_up_b[i], ffn_down_b[i], lng2, lnb2, alpha, 1024)
            if with_ctx:
                ctx2 = _ffn(res_c[0], mod_ctx, ffn_up_b[i], ffn_down_b[i], lng2, lnb2, alpha, n_ctx)
        else:
            x1, h2, lg = res
            tok = h2.reshape(bsz * seq, d)
            lgs = lg.reshape(bsz * seq, 128)
            if with_ctx:
                x1c, h2c, lgc = res_c
                tok = jnp.concatenate([tok, h2c.reshape(n_ctx, d)], axis=0)
                lgs = jnp.concatenate([lgs, lgc.reshape(n_ctx, 128)], axis=0)
            info, cnt = _route(lgs)
            dest, blk_expert, n_valid, n_pad = _moe_plan(info, cnt[0])
            x_pad = _scatter_rows(dest, tok, n_pad)
            y_pad = _experts(x_pad, blk_expert, n_valid, moe_up_b[i], moe_down_b[i])
            n_lat = bsz * seq
            x = _combine(dest[:n_lat], info[:n_lat], x1, mod_lat, lng2, lnb2, y_pad, alpha)
            if with_ctx:
                ctx2 = _combine(dest[n_lat:], info[n_lat:], x1c, mod_ctx, lng2, lnb2, y_pad, alpha)
    return x
```

```python
import functools
import math

import jax
import jax.numpy as jnp
from jax import lax
from jax.experimental import pallas as pl
from jax.experimental.pallas import tpu as pltpu

F32 = jnp.float32
BF16 = jnp.bfloat16
HI = lax.Precision.HIGHEST

D_MODEL = 1024
GRID_W = 64
HEAD_DIM = 64
ATTN_SCALE = HEAD_DIM ** -0.5
ROPE_THETA = 10000.0
NEG_INF = -1e30
NA_HEADS = 8
NA_KH = 8
NA_KW = 16
SWA_WINDOW = 128
SSM_HEADS = 8
SSM_INNER = 512
SSM_STATE = 128
SSM_CONV = 5
N_BRANCH = 4
BRANCH_W = 512
D_FF = 2816
N_EXPERTS = 8
D_FF_EXPERT = 3584
MOE_BLOCK = 512
LN_EPS = 1e-5
RMS_EPS = 1e-6

P_COLS = 4608
C512_NAQ, C512_NAK, C512_NAV, C512_SWQ, C512_GAQ, C512_Z = 0, 1, 2, 3, 4, 5
C1024_XBC = 3
C128_SWK, C128_SWV, C128_GAK, C128_GAV = 32, 33, 34, 35
DT_COLS = 256

VMEM_LIMIT = 48 * 1024 * 1024


def _cp(*sem):
    return pltpu.CompilerParams(dimension_semantics=sem, vmem_limit_bytes=VMEM_LIMIT)


def _sigmoid(x):
    return 1.0 / (1.0 + jnp.exp(-x))


def _silu(x):
    return x * _sigmoid(x)


def _softplus(x):
    return jnp.maximum(x, 0.0) + jnp.log1p(jnp.exp(-jnp.abs(x)))


def _layer_norm(y, g, b):
    mu = jnp.mean(y, axis=-1, keepdims=True)
    yc = y - mu
    var = jnp.mean(yc * yc, axis=-1, keepdims=True)
    return yc * lax.rsqrt(var + LN_EPS) * g + b


def _dot(a, b):
    return jnp.dot(a, b, preferred_element_type=F32)


def _dot_nt(a, b):
    return lax.dot_general(a, b, (((1,), (1,)), ((), ())), preferred_element_type=F32)


def _mod_kernel(s_ref, w_ref, b_ref, o_ref):
    s = _silu(s_ref[...])
    o_ref[0] = jnp.dot(s, w_ref[0], precision=HI, preferred_element_type=F32) + b_ref[0]


def _modulation(s_rows, w_mod, b_mod):
    n_l, d, d6 = w_mod.shape
    tn = 1024
    return pl.pallas_call(
        _mod_kernel,
        grid=(n_l, d6 // tn),
        in_specs=[pl.BlockSpec((8, d), lambda l, j: (0, 0)),
                  pl.BlockSpec((1, d, tn), lambda l, j: (l, 0, j)),
                  pl.BlockSpec((1, 1, tn), lambda l, j: (l, 0, j))],
        out_specs=pl.BlockSpec((1, 8, tn), lambda l, j: (l, 0, j)),
        out_shape=jax.ShapeDtypeStruct((n_l, 8, d6), F32),
        compiler_params=_cp("parallel", "parallel"),
        name="modulation",
    )(s_rows, w_mod, b_mod.reshape(n_l, 1, d6))


def _inproj_kernel(x_ref, mod_ref, w_ref, wdt_ref, p_ref, dt_ref, h_scr):
    @pl.when(pl.program_id(2) == 0)
    def _():
        m = mod_ref[0]
        h = (x_ref[0] * (1.0 + m[1:2]) + m[0:1]).astype(BF16)
        h_scr[...] = h
        dt_ref[0] = _dot(h, wdt_ref[...])

    p_ref[0] = _dot(h_scr[...], w_ref[...]).astype(BF16)


def _inproj(x3, mod, w_main, w_dt, tm):
    g, r, d = x3.shape
    tn = 1152
    return pl.pallas_call(
        _inproj_kernel,
        grid=(g, r // tm, P_COLS // tn),
        in_specs=[pl.BlockSpec((1, tm, d), lambda b, i, j: (b, i, 0)),
                  pl.BlockSpec((1, 8, d), lambda b, i, j: (b, 0, 0)),
                  pl.BlockSpec((d, tn), lambda b, i, j: (0, j)),
                  pl.BlockSpec((d, DT_COLS), lambda b, i, j: (0, 0))],
        out_specs=[pl.BlockSpec((1, tm, tn), lambda b, i, j: (b, i, j)),
                   pl.BlockSpec((1, tm, DT_COLS), lambda b, i, j: (b, i, 0))],
        out_shape=[jax.ShapeDtypeStruct((g, r, P_COLS), BF16),
                   jax.ShapeDtypeStruct((g, r, DT_COLS), F32)],
        scratch_shapes=[pltpu.VMEM((tm, d), BF16)],
        compiler_params=_cp("parallel", "parallel", "arbitrary"),
        name="inproj",
    )(x3, mod, w_main, w_dt)


GA_TK = 512
LOG2E = math.log2(math.e)


def _prep_kernel(swq_ref, gaq_ref, swk_ref, gak_ref, gav_ref, cos_ref, sa_ref, sb_ref, gq_ref, gk_ref,
                 oswq, ogaq, oswk, ogak, ogavt, *, rope, pad_q, vt_chunk):
    row = lax.broadcasted_iota(jnp.int32, (128, 128), 0) // HEAD_DIM
    col = lax.broadcasted_iota(jnp.int32, (128, 128), 1) // HEAD_DIM
    head_mean = jnp.where(row == col, 1.0 / HEAD_DIM, 0.0).astype(F32)
    first_half = lax.broadcasted_iota(jnp.int32, (1, 128), 1) < HEAD_DIM

    def rot(x):
        if not rope:
            return x
        return (x * cos_ref[...] + pltpu.roll(x, 128 - HEAD_DIM // 2, 1) * sa_ref[...]
                + pltpu.roll(x, HEAD_DIM // 2, 1) * sb_ref[...])

    def rms(x, g):
        ms = jnp.dot(x * x, head_mean, precision=HI, preferred_element_type=F32)
        return x * lax.rsqrt(ms + RMS_EPS) * g

    def put_padded(o_ref, x, c):
        swapped = pltpu.roll(x, HEAD_DIM, 1)
        keep = first_half if c < 2 else jnp.logical_not(first_half)
        even, odd = (x, swapped) if c < 2 else (swapped, x)
        o_ref[0, :, 2 * c * 128:(2 * c + 1) * 128] = jnp.where(keep, even, 0.0).astype(BF16)
        o_ref[0, :, (2 * c + 1) * 128:(2 * c + 2) * 128] = jnp.where(keep, odd, 0.0).astype(BF16)

    for c in range(4):
        sl = slice(c * 128, (c + 1) * 128)
        sq_c = rot(swq_ref[0, :, sl].astype(F32)) * ATTN_SCALE
        gq_c = rot(rms(gaq_ref[0, :, sl].astype(F32), gq_ref[...]))
        if pad_q:
            put_padded(oswq, sq_c, c)
            put_padded(ogaq, gq_c * (ATTN_SCALE * LOG2E), c)
        else:
            oswq[0, :, sl] = sq_c.astype(BF16)
            ogaq[0, :, sl] = (gq_c * ATTN_SCALE).astype(BF16)
    oswk[0] = rot(swk_ref[0].astype(F32)).astype(BF16)
    ogak[0] = rot(rms(gak_ref[0].astype(F32), gk_ref[...])).astype(BF16)
    for cc in range(gav_ref.shape[1] // vt_chunk):
        ogavt[0, cc] = gav_ref[0, cc * vt_chunk:(cc + 1) * vt_chunk, :].astype(F32).T.astype(BF16)


def _prep(p, tables, gq, gk, tm, rope, pad_q, vt_chunk):
    g, r, _ = p.shape
    cosf, sin_a, sin_b = tables
    tab = pl.BlockSpec((tm, 128), lambda b, i: (i, 0))
    vec = pl.BlockSpec((1, 128), lambda b, i: (0, 0))
    qw = 1024 if pad_q else 512
    n_vt = tm // vt_chunk
    return pl.pallas_call(
        functools.partial(_prep_kernel, rope=rope, pad_q=pad_q, vt_chunk=vt_chunk),
        grid=(g, r // tm),
        in_specs=[pl.BlockSpec((1, tm, 512), lambda b, i: (b, i, C512_SWQ)),
                  pl.BlockSpec((1, tm, 512), lambda b, i: (b, i, C512_GAQ)),
                  pl.BlockSpec((1, tm, 128), lambda b, i: (b, i, C128_SWK)),
                  pl.BlockSpec((1, tm, 128), lambda b, i: (b, i, C128_GAK)),
                  pl.BlockSpec((1, tm, 128), lambda b, i: (b, i, C128_GAV)),
                  tab, tab, tab, vec, vec],
        out_specs=[pl.BlockSpec((1, tm, qw), lambda b, i: (b, i, 0)),
                   pl.BlockSpec((1, tm, qw), lambda b, i: (b, i, 0)),
                   pl.BlockSpec((1, tm, 128), lambda b, i: (b, i, 0)),
                   pl.BlockSpec((1, tm, 128), lambda b, i: (b, i, 0)),
                   pl.BlockSpec((1, n_vt, 128, vt_chunk), lambda b, i: (b, i, 0, 0))],
        out_shape=[jax.ShapeDtypeStruct((g, r, qw), BF16), jax.ShapeDtypeStruct((g, r, qw), BF16),
                   jax.ShapeDtypeStruct((g, r, 128), BF16), jax.ShapeDtypeStruct((g, r, 128), BF16),
                   jax.ShapeDtypeStruct((g, r // vt_chunk, 128, vt_chunk), BF16)],
        compiler_params=_cp("parallel", "parallel"),
        name="qk_prep",
    )(p, p, p, p, p, cosf, sin_a, sin_b, gq, gk)


def _rope_tables(n_tok):
    t = jnp.arange(n_tok, dtype=jnp.int32)
    row = (t // GRID_W).astype(F32)
    col = (t % GRID_W).astype(F32)
    n_freq = HEAD_DIM // 4
    inv_freq = ROPE_THETA ** (-jnp.arange(n_freq, dtype=F32) / n_freq)
    ang = jnp.concatenate([row[:, None] * inv_freq, col[:, None] * inv_freq], axis=-1)
    cos, sin = jnp.cos(ang), jnp.sin(ang)
    zero = jnp.zeros_like(sin)
    cosf = jnp.tile(jnp.concatenate([cos, cos], -1), (1, 2))
    sin_a = jnp.tile(jnp.concatenate([-sin, zero], -1), (1, 2))
    sin_b = jnp.tile(jnp.concatenate([zero, sin], -1), (1, 2))
    return cosf, sin_a, sin_b


def _conv_kernel(prev_ref, cur_ref, next_ref, w_ref, b_ref, o_ref, *, tc):
    i = pl.program_id(1)
    n = pl.num_programs(1)
    prev = jnp.where(i > 0, prev_ref[0].astype(F32), 0.0)
    nxt = jnp.where(i < n - 1, next_ref[0].astype(F32), 0.0)
    ext = jnp.concatenate([prev, cur_ref[0].astype(F32), nxt], axis=0)
    rows = tc + 32
    acc = b_ref[...] + jnp.zeros((tc, ext.shape[1]), F32)
    for j in range(SSM_CONV):
        shift = (SSM_CONV // 2 - j) % rows
        sh = ext if shift == 0 else pltpu.roll(ext, shift, 0)
        acc = acc + sh[16:16 + tc] * w_ref[j:j + 1, :]
    o_ref[0] = _silu(acc).astype(BF16)


def _conv(p, w8, b, tc):
    g, r, _ = p.shape
    nb16 = r // 16
    k = tc // 16
    return pl.pallas_call(
        functools.partial(_conv_kernel, tc=tc),
        grid=(g, r // tc),
        in_specs=[pl.BlockSpec((1, 16, 1024), lambda b_, i: (b_, jnp.maximum(i * k - 1, 0), C1024_XBC)),
                  pl.BlockSpec((1, tc, 1024), lambda b_, i: (b_, i, C1024_XBC)),
                  pl.BlockSpec((1, 16, 1024), lambda b_, i: (b_, jnp.minimum((i + 1) * k, nb16 - 1), C1024_XBC)),
                  pl.BlockSpec((8, 1024), lambda b_, i: (0, 0)),
                  pl.BlockSpec((1, 1024), lambda b_, i: (0, 0))],
        out_specs=pl.BlockSpec((1, tc, 1024), lambda b_, i: (b_, i, 0)),
        out_shape=jax.ShapeDtypeStruct((g, r, 1024), BF16),
        compiler_params=_cp("parallel", "parallel"),
        name="dwconv_silu",
    )(p, p, p, w8, b)


def _softmax_out(units):
    probs = []
    for parts, sink in units:
        m = parts[0][0].max(axis=-1, keepdims=True)
        for s, _ in parts[1:]:
            m = jnp.maximum(m, s.max(axis=-1, keepdims=True))
        if sink is not None:
            m = jnp.maximum(m, sink)
        l = jnp.exp(sink - m) if sink is not None else 0.0
        es = []
        for s, _ in parts:
            e = jnp.exp(s - m)
            l = l + e.sum(axis=-1, keepdims=True)
            es.append(e.astype(BF16))
        probs.append((es, l))
    outs = []
    for (parts, _), (es, l) in zip(units, probs):
        acc = _dot(es[0], parts[0][1])
        for e, (_, v) in zip(es[1:], parts[1:]):
            acc = acc + _dot(e, v)
        outs.append(acc / l)
    return outs


NA_ROWS_PER_STEP = 4


def _na_kernel(q_ref, k_ref, v_ref, kc_ref, vc_ref, bias_ref, o_ref, *, n_rows):
    blk = pl.program_id(2)
    kc = kc_ref[0]
    vc = vc_ref[0]
    first = lax.broadcasted_iota(jnp.int32, (1, 128), 1) < HEAD_DIM
    keeps = (first, jnp.logical_not(first))
    vcs = [jnp.where(keep, vc, 0) for keep in keeps]

    units = []
    for rr in range(NA_ROWS_PER_STEP):
        r = blk * NA_ROWS_PER_STEP + rr
        start = jnp.clip(r - NA_KH // 2, 0, n_rows - NA_KH)
        off = start - r + NA_KH - 1
        q = q_ref[0, rr * GRID_W:(rr + 1) * GRID_W, :] * ATTN_SCALE
        k0 = pl.multiple_of(start * GRID_W, GRID_W)
        kw = k_ref[0, pl.ds(k0, NA_KH * GRID_W), :]
        vw = v_ref[0, pl.ds(k0, NA_KH * GRID_W), :]
        for hh in range(2):
            qh = jnp.where(keeps[hh], q, 0)
            s_loc = _dot_nt(qh, kw) + bias_ref[off, hh]
            s_ctx = _dot_nt(qh, kc)
            units.append(([(s_loc, jnp.where(keeps[hh], vw, 0)), (s_ctx, vcs[hh])], None))
    outs = _softmax_out(units)
    for rr in range(NA_ROWS_PER_STEP):
        o_ref[0, rr * GRID_W:(rr + 1) * GRID_W, :] = (outs[2 * rr] + outs[2 * rr + 1]).astype(BF16)


def _na_bias_table(rpb):
    o = jnp.arange(NA_KH)
    kr = jnp.arange(NA_KH)
    dr = o[:, None] + kr[None, :]
    c = jnp.arange(GRID_W)
    kcol = jnp.arange(GRID_W)
    cs = jnp.clip(c - NA_KW // 2, 0, GRID_W - NA_KW)
    dc = kcol[None, :] - c[:, None] + NA_KW - 1
    valid = (kcol[None, :] >= cs[:, None]) & (kcol[None, :] < cs[:, None] + NA_KW)
    tbl = rpb[:, dr][:, :, :, jnp.clip(dc, 0, 2 * NA_KW - 2)]
    tbl = jnp.where(valid[None, None, None], tbl.astype(F32), NEG_INF)
    return tbl.transpose(1, 0, 3, 2, 4).reshape(NA_KH, NA_HEADS, GRID_W, NA_KH * GRID_W)


def _na_attention(p, pc, bias):
    b, s, _ = p.shape
    lc = pc.shape[1]
    n_rows = s // GRID_W
    tq = NA_ROWS_PER_STEP * GRID_W
    return pl.pallas_call(
        functools.partial(_na_kernel, n_rows=n_rows),
        grid=(b, NA_HEADS // 2, s // tq),
        in_specs=[pl.BlockSpec((1, tq, 128), lambda b_, h, i: (b_, i, 4 * C512_NAQ + h)),
                  pl.BlockSpec((1, s, 128), lambda b_, h, i: (b_, 0, 4 * C512_NAK + h)),
                  pl.BlockSpec((1, s, 128), lambda b_, h, i: (b_, 0, 4 * C512_NAV + h)),
                  pl.BlockSpec((1, lc, 128), lambda b_, h, i: (b_, 0, 4 * C512_NAK + h)),
                  pl.BlockSpec((1, lc, 128), lambda b_, h, i: (b_, 0, 4 * C512_NAV + h)),
                  pl.BlockSpec((NA_KH, 2, GRID_W, NA_KH * GRID_W), lambda b_, h, i: (0, h, 0, 0))],
        out_specs=pl.BlockSpec((1, tq, 128), lambda b_, h, i: (b_, i, h)),
        out_shape=jax.ShapeDtypeStruct((b, s, 512), BF16),
        compiler_params=_cp("parallel", "parallel", "arbitrary"),
        name="na_attention",
    )(p, p, p, pc, pc, bias)


SWA_TQ = 256
SWA_BAND = SWA_TQ + 2 * SWA_WINDOW


def _swa_kernel(q_ref, k_ref, v_ref, kc_ref, vc_ref, sink_ref, o_ref, *, n_tok):
    n = pl.program_id(1)
    start = pl.multiple_of(jnp.clip(n * SWA_TQ - SWA_WINDOW, 0, n_tok - SWA_BAND), SWA_WINDOW)
    kb = k_ref[0, pl.ds(start, SWA_BAND), :]
    vb = v_ref[0, pl.ds(start, SWA_BAND), :]
    kc = kc_ref[0]
    vc = vc_ref[0]
    qpos = n * SWA_TQ + lax.broadcasted_iota(jnp.int32, (SWA_TQ, SWA_BAND), 0)
    kpos = start + lax.broadcasted_iota(jnp.int32, (SWA_TQ, SWA_BAND), 1)
    in_win = jnp.abs(kpos - qpos) <= SWA_WINDOW
    first = lax.broadcasted_iota(jnp.int32, (1, 128), 1) < HEAD_DIM
    keeps = (first, jnp.logical_not(first))

    def value_variants(v):
        sw = pltpu.roll(v.astype(F32), HEAD_DIM, 1).astype(BF16)
        return [[jnp.where(keeps[pos], v if pos == g else sw, 0) for pos in range(2)] for g in range(2)]

    vbs = value_variants(vb)
    vcs = value_variants(vc)
    units = []
    for h in range(8):
        g, pos = h // 4, h % 2
        qh = q_ref[0, :, h * 128:(h + 1) * 128]
        s_loc = jnp.where(in_win, _dot_nt(qh, kb), NEG_INF)
        s_ctx = _dot_nt(qh, kc)
        units.append(([(s_loc, vbs[g][pos]), (s_ctx, vcs[g][pos])], sink_ref[h:h + 1, 0:1]))
    outs = _softmax_out(units)
    o_ref[0] = jnp.concatenate([outs[2 * j] + outs[2 * j + 1] for j in range(4)], axis=-1).astype(BF16)


def _swa_attention(q, k, p, kc, pc, sink_tab):
    b, s, _ = k.shape
    lc = kc.shape[1]
    return pl.pallas_call(
        functools.partial(_swa_kernel, n_tok=s),
        grid=(b, s // SWA_TQ),
        in_specs=[pl.BlockSpec((1, SWA_TQ, 1024), lambda b_, i: (b_, i, 0)),
                  pl.BlockSpec((1, s, 128), lambda b_, i: (b_, 0, 0)),
                  pl.BlockSpec((1, s, 128), lambda b_, i: (b_, 0, C128_SWV)),
                  pl.BlockSpec((1, lc, 128), lambda b_, i: (b_, 0, 0)),
                  pl.BlockSpec((1, lc, 128), lambda b_, i: (b_, 0, C128_SWV)),
                  pl.BlockSpec((8, 128), lambda b_, i: (0, 0))],
        out_specs=pl.BlockSpec((1, SWA_TQ, 512), lambda b_, i: (b_, i, 0)),
        out_shape=jax.ShapeDtypeStruct((b, s, 512), BF16),
        compiler_params=_cp("parallel", "arbitrary"),
        name="swa_attention",
    )(q, k, p, kc, pc, sink_tab)


GA_TQ = 256


def _ga_kernel(q_ref, k_ref, vt_ref, kc_ref, vct_ref, o_ref, acc_scr, *, n_chunks):
    acc_scr[...] = jnp.zeros_like(acc_scr)

    def step(carry, kk, vt):
        m_all, l_all = carry

        def scores(h):
            return _dot_nt(kk, q_ref[0, :, h * 128:(h + 1) * 128])

        m_rows, l_rows = [], []
        sts = [scores(h) for h in range(8)]
        for h in range(8):
            st = sts[h]
            g = h // 4
            m_old = m_all[h:h + 1, :]
            m_new = jnp.maximum(m_old, st.max(axis=0, keepdims=True))
            a = jnp.exp2(m_old - m_new)
            e = jnp.exp2(st - m_new)
            l_rows.append(a * l_all[h:h + 1, :] + e.sum(axis=0, keepdims=True))
            m_rows.append(m_new)
            rows = slice(h * HEAD_DIM, (h + 1) * HEAD_DIM)
            acc_scr[rows, :] = a * acc_scr[rows, :] + _dot(vt[g * HEAD_DIM:(g + 1) * HEAD_DIM, :], e.astype(BF16))
        return jnp.concatenate(m_rows, axis=0), jnp.concatenate(l_rows, axis=0)

    def chunk(c, carry):
        k0 = pl.multiple_of(c * GA_TK, GA_TK)
        return step(carry, k_ref[0, pl.ds(k0, GA_TK), :], vt_ref[0, c])

    init = (jnp.full((8, GA_TQ), NEG_INF, F32), jnp.zeros((8, GA_TQ), F32))
    carry = lax.fori_loop(0, n_chunks, chunk, init)
    _, l_fin = step(carry, kc_ref[0], vct_ref[0, 0])
    inv = 1.0 / l_fin
    out_t = jnp.concatenate([acc_scr[h * HEAD_DIM:(h + 1) * HEAD_DIM, :] * inv[h:h + 1, :] for h in range(8)], axis=0)
    o_ref[0] = out_t.T.astype(BF16)


def _ga_attention(q_pad, k, vt, kc, vct):
    b, s, _ = k.shape
    lc = kc.shape[1]
    n_chunks = s // GA_TK
    return pl.pallas_call(
        functools.partial(_ga_kernel, n_chunks=n_chunks),
        grid=(b, s // GA_TQ),
        in_specs=[pl.BlockSpec((1, GA_TQ, 1024), lambda b_, i: (b_, i, 0)),
                  pl.BlockSpec((1, s, 128), lambda b_, i: (b_, 0, 0)),
                  pl.BlockSpec((1, n_chunks, 128, GA_TK), lambda b_, i: (b_, 0, 0, 0)),
                  pl.BlockSpec((1, lc, 128), lambda b_, i: (b_, 0, 0)),
                  pl.BlockSpec((1, 1, 128, lc), lambda b_, i: (b_, 0, 0, 0))],
        out_specs=pl.BlockSpec((1, GA_TQ, 512), lambda b_, i: (b_, i, 0)),
        out_shape=jax.ShapeDtypeStruct((b, s, 512), BF16),
        scratch_shapes=[pltpu.VMEM((8 * HEAD_DIM, GA_TQ), F32)],
        compiler_params=_cp("parallel", "arbitrary"),
        name="ga_attention",
    )(q_pad, k, vt, kc, vct)


def _ctx_attn_kernel(q_ref, k_ref, v_ref, sink_ref, o_ref, *, n_kv, q_scale, use_sink):
    rep = 8 // n_kv
    units = []
    for h in range(8):
        g = h // rep
        gs = slice(g * HEAD_DIM, (g + 1) * HEAD_DIM)
        qh = q_ref[0, :, h * HEAD_DIM:(h + 1) * HEAD_DIM]
        if q_scale:
            qh = qh * ATTN_SCALE
        s = _dot_nt(qh, k_ref[0, :, gs])
        units.append(([(s, v_ref[0, :, gs])], sink_ref[h:h + 1, 0:1] if use_sink else None))
    o_ref[0] = jnp.concatenate(_softmax_out(units), axis=-1).astype(BF16)


def _ctx_attention(q, qcol, k, kcol, v, vcol, n_kv, sink_tab, q_scale, use_sink):
    b, lc, _ = q.shape
    kvw = n_kv * HEAD_DIM
    return pl.pallas_call(
        functools.partial(_ctx_attn_kernel, n_kv=n_kv, q_scale=q_scale, use_sink=use_sink),
        grid=(b,),
        in_specs=[pl.BlockSpec((1, lc, 512), lambda b_: (b_, 0, qcol)),
                  pl.BlockSpec((1, lc, kvw), lambda b_: (b_, 0, kcol)),
                  pl.BlockSpec((1, lc, kvw), lambda b_: (b_, 0, vcol)),
                  pl.BlockSpec((8, 128), lambda b_: (0, 0))],
        out_specs=pl.BlockSpec((1, lc, 512), lambda b_: (b_, 0, 0)),
        out_shape=jax.ShapeDtypeStruct((b, lc, 512), BF16),
        compiler_params=_cp("parallel"),
        name="ctx_attention",
    )(q, k, v, sink_tab)


def _ssd_kernel(u_ref, dt_ref, dtb_ref, a_ref, h0_ref, y_ref, hT_ref, st, *, q_len):
    d = pl.program_id(1)
    c = pl.program_id(2)

    @pl.when(c == 0)
    def _():
        st[...] = h0_ref[0, 0]

    u = u_ref[0]
    dt = _softplus(dt_ref[0] + dtb_ref[...])
    da = dt * a_ref[...]
    row = lax.broadcasted_iota(jnp.int32, (q_len, q_len), 0)
    col = lax.broadcasted_iota(jnp.int32, (q_len, q_len), 1)
    causal = (row - col) * (1 - 2 * d) >= 0
    acum = jnp.dot(causal.astype(F32), da, precision=HI, preferred_element_type=F32)
    acum_t = acum.T
    total = jnp.sum(da, axis=0, keepdims=True)
    cbs, bg_ts, s_gs, y_offs = [], [], [], []
    for g in range(2):
        bg = u[:, SSM_INNER + g * SSM_STATE:SSM_INNER + (g + 1) * SSM_STATE]
        cg = u[:, SSM_INNER + (2 + g) * SSM_STATE:SSM_INNER + (3 + g) * SSM_STATE]
        cbs.append(_dot_nt(cg, bg))
        bg_ts.append(bg.astype(F32).T.astype(BF16))
        s_gs.append(st[g])
        y_offs.append(_dot(cg, s_gs[g].astype(BF16)))
    first = lax.broadcasted_iota(jnp.int32, (1, 128), 1) < HEAD_DIM
    mixes, xms, xdecs, e_acs, e_tots = [], [], [], [], []
    for pr in range(SSM_HEADS // 2):
        h0, h1 = 2 * pr, 2 * pr + 1
        ac0, ac1 = acum[:, h0:h0 + 1], acum[:, h1:h1 + 1]
        ac_pair = jnp.where(first, ac0, ac1)
        tot_pair = jnp.where(first, total[:, h0:h0 + 1], total[:, h1:h1 + 1])
        xdt = u[:, pr * 128:(pr + 1) * 128].astype(F32) * jnp.where(first, dt[:, h0:h0 + 1], dt[:, h1:h1 + 1])
        for h, ac, keep in ((h0, ac0, first), (h1, ac1, jnp.logical_not(first))):
            seg = jnp.exp(jnp.where(causal, ac - acum_t[h:h + 1, :], NEG_INF))
            mixes.append((cbs[pr // 2] * seg).astype(BF16))
            xms.append(jnp.where(keep, xdt, 0.0).astype(BF16))
        xdecs.append((xdt * jnp.exp(tot_pair - ac_pair)).astype(BF16))
        e_acs.append(jnp.exp(ac_pair))
        e_tots.append(jnp.exp(tot_pair))
    y_diag = [_dot(mixes[2 * pr], xms[2 * pr]) + _dot(mixes[2 * pr + 1], xms[2 * pr + 1])
              for pr in range(SSM_HEADS // 2)]
    upd = [_dot(bg_ts[pr // 2], xdecs[pr]) for pr in range(SSM_HEADS // 2)]
    ys = []
    for g in range(2):
        new_cols = []
        for pp in range(2):
            pr = 2 * g + pp
            ls = slice(pp * 128, (pp + 1) * 128)
            ys.append(y_diag[pr] + y_offs[g][:, ls] * e_acs[pr])
            new_cols.append(s_gs[g][:, ls] * e_tots[pr] + upd[pr])
        st[g] = jnp.concatenate(new_cols, axis=-1)
    y_ref[0, 0] = jnp.concatenate(ys, axis=-1)

    @pl.when(c == pl.num_programs(2) - 1)
    def _():
        hT_ref[0, 0] = st[...]


def _ssd(u, dt_raw, dtb, a_full, h0, q_len):
    b, l, _ = u.shape
    n_c = l // q_len

    def cidx(d, c):
        return c + d * (n_c - 1 - 2 * c)

    return pl.pallas_call(
        functools.partial(_ssd_kernel, q_len=q_len),
        grid=(b, 2, n_c),
        in_specs=[pl.BlockSpec((1, q_len, 1024), lambda b_, d, c: (b_, cidx(d, c), 0)),
                  pl.BlockSpec((1, q_len, 128), lambda b_, d, c: (b_, cidx(d, c), d)),
                  pl.BlockSpec((1, 128), lambda b_, d, c: (0, d)),
                  pl.BlockSpec((1, 128), lambda b_, d, c: (0, d)),
                  pl.BlockSpec((1, 1, 2, SSM_STATE, 256), lambda b_, d, c: (b_, d, 0, 0, 0))],
        out_specs=[pl.BlockSpec((1, 1, q_len, 512), lambda b_, d, c: (d, b_, cidx(d, c), 0)),
                   pl.BlockSpec((1, 1, 2, SSM_STATE, 256), lambda b_, d, c: (b_, d, 0, 0, 0))],
        out_shape=[jax.ShapeDtypeStruct((2, b, l, 512), F32),
                   jax.ShapeDtypeStruct((b, 2, 2, SSM_STATE, 256), F32)],
        scratch_shapes=[pltpu.VMEM((2, SSM_STATE, 256), F32)],
        compiler_params=_cp("parallel", "arbitrary", "arbitrary"),
        name="ssd_scan",
    )(u, dt_raw, dtb, a_full, h0)


def _ssd_finish_kernel(yf_ref, yb_ref, xs_ref, z_ref, dsk_ref, g_ref, o_ref):
    y = yf_ref[0, 0] + yb_ref[0, 0] + dsk_ref[...] * xs_ref[0].astype(F32)
    y = y * _silu(z_ref[0].astype(F32))
    ms = jnp.mean(y * y, axis=-1, keepdims=True)
    o_ref[0] = (y * lax.rsqrt(ms + RMS_EPS) * g_ref[...]).astype(BF16)


def _ssd_finish(y, u, p, dsk, g, tm):
    _, b, l, _ = y.shape
    vec = pl.BlockSpec((1, 512), lambda b_, i: (0, 0))
    return pl.pallas_call(
        _ssd_finish_kernel,
        grid=(b, l // tm),
        in_specs=[pl.BlockSpec((1, 1, tm, 512), lambda b_, i: (0, b_, i, 0)),
                  pl.BlockSpec((1, 1, tm, 512), lambda b_, i: (1, b_, i, 0)),
                  pl.BlockSpec((1, tm, 512), lambda b_, i: (b_, i, 0)),
                  pl.BlockSpec((1, tm, 512), lambda b_, i: (b_, i, C512_Z)),
                  vec, vec],
        out_specs=pl.BlockSpec((1, tm, 512), lambda b_, i: (b_, i, 0)),
        out_shape=jax.ShapeDtypeStruct((b, l, 512), BF16),
        compiler_params=_cp("parallel", "parallel"),
        name="ssd_finish",
    )(y, y, u, p, dsk, g)


def _merge_kernel(*refs, alpha, moe):
    if moe:
        (x_ref, mod_ref, oa, ob, oc, od, wg_ref, bg_ref, wb_ref, wo_ref, lng_ref, lnb_ref, wr_ref, br_ref,
         x1_ref, h2_ref, lg_ref, h_scr, acc_scr) = refs
    else:
        (x_ref, mod_ref, oa, ob, oc, od, wg_ref, bg_ref, wb_ref, wo_ref, lng_ref, lnb_ref,
         x1_ref, h_scr, acc_scr) = refs
    i = pl.program_id(2)
    m = mod_ref[0]

    @pl.when(i == 0)
    def _():
        h_scr[...] = (x_ref[0] * (1.0 + m[1:2]) + m[0:1]).astype(BF16)
        acc_scr[...] = jnp.zeros_like(acc_scr)

    for k, br in enumerate((oa, ob, oc, od)):
        @pl.when(i == k)
        def _(br=br):
            gate = _sigmoid(_dot(h_scr[...], wg_ref[0]) + bg_ref[0])
            acc_scr[...] += gate * _dot(br[0], wb_ref[0])

    @pl.when(i == N_BRANCH - 1)
    def _():
        o = _dot(acc_scr[...].astype(BF16), wo_ref[...])
        x1 = _layer_norm(alpha * x_ref[0] + m[2:3] * o, lng_ref[...], lnb_ref[...])
        x1_ref[0] = x1
        if moe:
            h2 = x1 * (1.0 + m[4:5]) + m[3:4]
            h2_ref[0] = h2
            lg_ref[0] = jnp.dot(h2, wr_ref[...], precision=HI, preferred_element_type=F32) + br_ref[...]


def _merge(x3, mod, branches, wg, bg, wb, wo, lng, lnb, alpha, tm, router=None):
    g, r, d = x3.shape
    moe = router is not None
    tile = lambda w: pl.BlockSpec((1, tm, w), lambda b, t, i: (b, t, 0))
    vec = pl.BlockSpec((1, d), lambda b, t, i: (0, 0))
    in_specs = [tile(d), pl.BlockSpec((1, 8, d), lambda b, t, i: (b, 0, 0)),
                tile(BRANCH_W), tile(BRANCH_W), tile(BRANCH_W), tile(BRANCH_W),
                pl.BlockSpec((1, d, d), lambda b, t, i: (i, 0, 0)),
                pl.BlockSpec((1, 1, d), lambda b, t, i: (i, 0, 0)),
                pl.BlockSpec((1, BRANCH_W, d), lambda b, t, i: (i, 0, 0)),
                pl.BlockSpec((d, d), lambda b, t, i: (0, 0)), vec, vec]
    args = [x3, mod, *branches, wg, bg, wb, wo, lng, lnb]
    out_specs = [tile(d)]
    out_shape = [jax.ShapeDtypeStruct((g, r, d), F32)]
    if moe:
        in_specs += [pl.BlockSpec((d, 128), lambda b, t, i: (0, 0)), pl.BlockSpec((1, 128), lambda b, t, i: (0, 0))]
        args += list(router)
        out_specs += [tile(d), tile(128)]
        out_shape += [jax.ShapeDtypeStruct((g, r, d), F32), jax.ShapeDtypeStruct((g, r, 128), F32)]
    return pl.pallas_call(
        functools.partial(_merge_kernel, alpha=alpha, moe=moe),
        grid=(g, r // tm, N_BRANCH),
        in_specs=in_specs, out_specs=out_specs, out_shape=out_shape,
        scratch_shapes=[pltpu.VMEM((tm, d), BF16), pltpu.VMEM((tm, d), F32)],
        compiler_params=_cp("parallel", "parallel", "arbitrary"),
        name="merge_ln1",
    )(*args)


FFN_CHUNK = D_FF // 2


def _ffn_kernel(x_ref, mod_ref, wg_ref, wu_ref, wd_ref, lng_ref, lnb_ref, o_ref, h_scr, acc_scr, *, alpha):
    j = pl.program_id(2)
    m = mod_ref[0]

    @pl.when(j == 0)
    def _():
        h_scr[...] = (x_ref[0] * (1.0 + m[4:5]) + m[3:4]).astype(BF16)
        acc_scr[...] = jnp.zeros_like(acc_scr)

    h = h_scr[...]
    a = _silu(_dot(h, wg_ref[...])) * _dot(h, wu_ref[...])
    acc_scr[...] += _dot(a.astype(BF16), wd_ref[...])

    @pl.when(j == pl.num_programs(2) - 1)
    def _():
        o_ref[0] = _layer_norm(alpha * x_ref[0] + m[5:6] * acc_scr[...], lng_ref[...], lnb_ref[...])


def _ffn(x3, mod, w_up, w_down, lng, lnb, alpha, tm):
    g, r, d = x3.shape
    n_j = D_FF // FFN_CHUNK
    vec = pl.BlockSpec((1, d), lambda b, t, j: (0, 0))
    return pl.pallas_call(
        functools.partial(_ffn_kernel, alpha=alpha),
        grid=(g, r // tm, n_j),
        in_specs=[pl.BlockSpec((1, tm, d), lambda b, t, j: (b, t, 0)),
                  pl.BlockSpec((1, 8, d), lambda b, t, j: (b, 0, 0)),
                  pl.BlockSpec((d, FFN_CHUNK), lambda b, t, j: (0, j)),
                  pl.BlockSpec((d, FFN_CHUNK), lambda b, t, j: (0, n_j + j)),
                  pl.BlockSpec((FFN_CHUNK, d), lambda b, t, j: (j, 0)), vec, vec],
        out_specs=pl.BlockSpec((1, tm, d), lambda b, t, j: (b, t, 0)),
        out_shape=jax.ShapeDtypeStruct((g, r, d), F32),
        scratch_shapes=[pltpu.VMEM((tm, d), BF16), pltpu.VMEM((tm, d), F32)],
        compiler_params=_cp("parallel", "parallel", "arbitrary"),
        name="ffn_ln2",
    )(x3, mod, w_up, w_up, w_down, lng, lnb)


ROUTE_TM = 256


def _route_kernel(lg_ref, info_ref, cnt_ref, carry):
    i = pl.program_id(0)

    @pl.when(i == 0)
    def _():
        carry[...] = jnp.zeros_like(carry)

    lg = lg_ref[...]
    tm = lg.shape[0]
    lane = lax.broadcasted_iota(jnp.int32, lg.shape, 1).astype(F32)
    m1 = lg.max(axis=-1, keepdims=True)
    i1 = jnp.where(lg == m1, lane, 128.0).min(axis=-1, keepdims=True)
    lg2 = jnp.where(lane == i1, -jnp.inf, lg)
    m2 = lg2.max(axis=-1, keepdims=True)
    i2 = jnp.where(lg2 == m2, lane, 128.0).min(axis=-1, keepdims=True)
    e = jnp.exp(m2 - m1)
    w1 = 1.0 / (1.0 + e)
    w2 = e / (1.0 + e)
    sel = jnp.where((lane == i1) | (lane == i2), 1.0, 0.0).astype(F32)
    row = lax.broadcasted_iota(jnp.int32, (tm, tm), 0)
    col = lax.broadcasted_iota(jnp.int32, (tm, tm), 1)
    before = jnp.where(row > col, 1.0, 0.0).astype(BF16)
    excl = _dot(before, sel.astype(BF16)) + carry[0:1, :]
    r1 = jnp.where(lane == i1, excl, 0.0).sum(axis=-1, keepdims=True)
    r2 = jnp.where(lane == i2, excl, 0.0).sum(axis=-1, keepdims=True)
    carry[...] = carry[...] + sel.sum(axis=0, keepdims=True)
    info = jnp.where(lane == 0, i1, 0.0)
    info = jnp.where(lane == 1, i2, info)
    info = jnp.where(lane == 2, r1, info)
    info = jnp.where(lane == 3, r2, info)
    info = jnp.where(lane == 4, w1, info)
    info = jnp.where(lane == 5, w2, info)
    info_ref[...] = info
    cnt_ref[...] = carry[...]


def _route(logits):
    t = logits.shape[0]
    return pl.pallas_call(
        _route_kernel,
        grid=(t // ROUTE_TM,),
        in_specs=[pl.BlockSpec((ROUTE_TM, 128), lambda i: (i, 0))],
        out_specs=[pl.BlockSpec((ROUTE_TM, 128), lambda i: (i, 0)), pl.BlockSpec((8, 128), lambda i: (0, 0))],
        out_shape=[jax.ShapeDtypeStruct((t, 128), F32), jax.ShapeDtypeStruct((8, 128), F32)],
        scratch_shapes=[pltpu.VMEM((8, 128), F32)],
        compiler_params=_cp("arbitrary"),
        name="moe_route",
    )(logits)


SCATTER_TM = 256


def _scatter_kernel(d_ref, tok_ref, zero_ref, out_ref, sem):
    del zero_ref

    def copy(t, k):
        return pltpu.make_async_copy(tok_ref.at[pl.ds(t, 1)], out_ref.at[pl.ds(d_ref[0, 0, 2 * t + k], 1)], sem)

    def issue(t, carry):
        copy(t, 0).start()
        copy(t, 1).start()
        return carry

    def drain(t, carry):
        copy(t, 0).wait()
        copy(t, 1).wait()
        return carry

    lax.fori_loop(0, SCATTER_TM, issue, 0, unroll=8)
    lax.fori_loop(0, SCATTER_TM, drain, 0, unroll=8)


def _scatter_rows(dest, tok, n_pad):
    t, d = tok.shape
    n_t = t // SCATTER_TM
    return pl.pallas_call(
        _scatter_kernel,
        grid=(n_t,),
        in_specs=[pl.BlockSpec((1, 1, 2 * SCATTER_TM), lambda i: (i, 0, 0), memory_space=pltpu.SMEM),
                  pl.BlockSpec((SCATTER_TM, d), lambda i: (i, 0)),
                  pl.BlockSpec(memory_space=pl.ANY)],
        out_specs=pl.BlockSpec(memory_space=pl.ANY),
        out_shape=jax.ShapeDtypeStruct((n_pad, d), F32),
        scratch_shapes=[pltpu.SemaphoreType.DMA(())],
        input_output_aliases={2: 0},
        compiler_params=_cp("arbitrary"),
        name="moe_scatter",
    )(dest.reshape(n_t, 1, 2 * SCATTER_TM), tok, jnp.zeros((n_pad, d), F32))


EXP_CHUNK = D_FF_EXPERT // 4


def _expert_kernel(be_ref, nv_ref, x_ref, wg_ref, wu_ref, wd_ref, y_ref):
    del be_ref
    b = pl.program_id(0)
    j = pl.program_id(1)

    @pl.when(j == 0)
    def _():
        y_ref[...] = jnp.zeros_like(y_ref)

    @pl.when(b < nv_ref[0])
    def _():
        h = x_ref[...].astype(BF16)
        a = _silu(_dot(h, wg_ref[0])) * _dot(h, wu_ref[0])
        y_ref[...] += _dot(a.astype(BF16), wd_ref[0])


def _experts(x_pad, blk_expert, n_valid, w_up, w_down):
    n_pad, d = x_pad.shape
    n_blk = n_pad // MOE_BLOCK
    n_j = D_FF_EXPERT // EXP_CHUNK

    def jj(b, j, nv):
        return jnp.where(b < nv[0], j, n_j - 1)

    grid_spec = pltpu.PrefetchScalarGridSpec(
        num_scalar_prefetch=2,
        grid=(n_blk, n_j),
        in_specs=[pl.BlockSpec((MOE_BLOCK, d), lambda b, j, be, nv: (b, 0)),
                  pl.BlockSpec((1, d, EXP_CHUNK), lambda b, j, be, nv: (be[b], 0, jj(b, j, nv))),
                  pl.BlockSpec((1, d, EXP_CHUNK), lambda b, j, be, nv: (be[b], 0, n_j + jj(b, j, nv))),
                  pl.BlockSpec((1, EXP_CHUNK, d), lambda b, j, be, nv: (be[b], jj(b, j, nv), 0))],
        out_specs=pl.BlockSpec((MOE_BLOCK, d), lambda b, j, be, nv: (b, 0)),
    )
    return pl.pallas_call(
        _expert_kernel,
        grid_spec=grid_spec,
        out_shape=jax.ShapeDtypeStruct((n_pad, d), F32),
        compiler_params=_cp("arbitrary", "arbitrary"),
        name="moe_experts",
    )(blk_expert, n_valid, x_pad, w_up, w_up, w_down)


COMBINE_TM = 256


def _combine_kernel(d_ref, info_ref, x_ref, mod_ref, lng_ref, lnb_ref, y_ref, o_ref, gbuf, sem, *, alpha):
    def copy(t, k):
        return pltpu.make_async_copy(y_ref.at[pl.ds(d_ref[0, 0, 2 * t + k], 1)], gbuf.at[k, pl.ds(t, 1)], sem)

    def issue(t, carry):
        copy(t, 0).start()
        copy(t, 1).start()
        return carry

    def drain(t, carry):
        copy(t, 0).wait()
        copy(t, 1).wait()
        return carry

    lax.fori_loop(0, COMBINE_TM, issue, 0, unroll=8)
    lax.fori_loop(0, COMBINE_TM, drain, 0, unroll=8)
    m = mod_ref[0]
    info = info_ref[0]
    f = info[:, 4:5] * gbuf[0] + info[:, 5:6] * gbuf[1]
    o_ref[0] = _layer_norm(alpha * x_ref[0] + m[5:6] * f, lng_ref[...], lnb_ref[...])


def _combine(dest, info, x3, mod, lng, lnb, y_pad, alpha):
    g, r, d = x3.shape
    n_t = r // COMBINE_TM
    vec = pl.BlockSpec((1, d), lambda b, i: (0, 0))
    return pl.pallas_call(
        functools.partial(_combine_kernel, alpha=alpha),
        grid=(g, n_t),
        in_specs=[pl.BlockSpec((1, 1, 2 * COMBINE_TM), lambda b, i: (b * n_t + i, 0, 0), memory_space=pltpu.SMEM),
                  pl.BlockSpec((1, COMBINE_TM, 128), lambda b, i: (b, i, 0)),
                  pl.BlockSpec((1, COMBINE_TM, d), lambda b, i: (b, i, 0)),
                  pl.BlockSpec((1, 8, d), lambda b, i: (b, 0, 0)), vec, vec,
                  pl.BlockSpec(memory_space=pl.ANY)],
        out_specs=pl.BlockSpec((1, COMBINE_TM, d), lambda b, i: (b, i, 0)),
        out_shape=jax.ShapeDtypeStruct((g, r, d), F32),
        scratch_shapes=[pltpu.VMEM((2, COMBINE_TM, d), F32), pltpu.SemaphoreType.DMA(())],
        compiler_params=_cp("arbitrary", "arbitrary"),
        name="moe_combine_ln2",
    )(dest.reshape(g * n_t, 1, 2 * COMBINE_TM), info.reshape(g, r, 128), x3, mod, lng, lnb, y_pad)


def _moe_plan(info, counts_row):
    n_tok = info.shape[0]
    e = info[:, 0:2].astype(jnp.int32)
    rank = info[:, 2:4].astype(jnp.int32)
    counts = counts_row[:N_EXPERTS].astype(jnp.int32)
    padded = (counts + MOE_BLOCK - 1) // MOE_BLOCK * MOE_BLOCK
    pad_end = jnp.cumsum(padded)
    pad_start = pad_end - padded
    dest = pad_start[e] + rank
    n_blk = -(-n_tok * 2 // MOE_BLOCK) + N_EXPERTS
    blk_start = jnp.arange(n_blk, dtype=jnp.int32) * MOE_BLOCK
    blk_expert = jnp.minimum(jnp.sum(pad_end[None, :] <= blk_start[:, None], axis=1), N_EXPERTS - 1).astype(jnp.int32)
    n_valid = (pad_end[-1:] // MOE_BLOCK).astype(jnp.int32)
    return dest, blk_expert, n_valid, n_blk * MOE_BLOCK


def _pad_rows(a, rows):
    return jnp.concatenate([a, jnp.zeros((rows - a.shape[0],) + a.shape[1:], a.dtype)], axis=0)


def kernel(x, c, ctx, c_ctx, w_mod, b_mod, w_in, w_gate, b_gate, na_rpb, swa_sink, qk_gain_q, qk_gain_k, conv_w, conv_b, dt_bias, a_log, d_skip, ssm_norm_g, w_branch, w_out, ln1_g, ln1_b, ln2_g, ln2_b, ffn_w_up, ffn_w_down, moe_w_router, moe_b_router, moe_w_up, moe_w_down):
    bsz, seq, d = x.shape
    lc = ctx.shape[1]
    depth = w_mod.shape[0]
    alpha = (2 * depth) ** 0.25
    n_ctx = bsz * lc
    assert bsz < 8 and seq % 1024 == 0 and lc % 256 == 0 and d == D_MODEL

    s_rows = _pad_rows(jnp.concatenate([c, c_ctx[None, :]], axis=0), 8)
    mods = _modulation(s_rows, w_mod, b_mod).reshape(depth, 8, 6, d)
    mods = jnp.concatenate([mods, jnp.zeros((depth, 8, 2, d), F32)], axis=2)

    offs = [0, 512, 1024, 1536, 2048, 2176, 2304, 2816, 2944, 3072, 3584, 4608, 4624]
    seg = lambda k: w_in[:, :, offs[k]:offs[k + 1]]
    w_main = jnp.concatenate([seg(0), seg(1), seg(2), seg(3), seg(6), seg(9), seg(10), seg(4), seg(5), seg(7), seg(8)],
                             axis=-1).astype(BF16)
    zpad = jnp.zeros((depth, d, 128 - SSM_HEADS), F32)
    w_dt = jnp.concatenate([seg(11)[:, :, :SSM_HEADS], zpad, seg(11)[:, :, SSM_HEADS:], zpad], axis=-1).astype(BF16)

    w_gate_b = w_gate.astype(BF16)
    w_branch_b = w_branch.astype(BF16)
    w_out_b = w_out.astype(BF16)
    ffn_up_b = ffn_w_up.astype(BF16)
    ffn_down_b = ffn_w_down.astype(BF16)
    moe_up_b = moe_w_up.astype(BF16)
    moe_down_b = moe_w_down.astype(BF16)

    rope_tabs = _rope_tables(seq)
    a_neg = -jnp.exp(a_log.astype(F32))
    pad8 = lambda v: jnp.concatenate([v, jnp.zeros(v.shape[:-1] + (128 - SSM_HEADS,), F32)], axis=-1)

    ctx2 = ctx.reshape(1, n_ctx, d)
    for l in range(depth):
        with_ctx = l < depth - 1
        mod_lat = mods[l, :bsz]
        mod_ctx = mods[l, bsz:bsz + 1]

        p, dt_raw = _inproj(x, mod_lat, w_main[l], w_dt[l], 1024)
        pc, dtc_raw = _inproj(ctx2, mod_ctx, w_main[l], w_dt[l], n_ctx)
        pc = pc.reshape(bsz, lc, P_COLS)
        dtc_raw = dtc_raw.reshape(bsz, lc, DT_COLS)

        gq = jnp.tile(qk_gain_q[l].astype(F32), 2)[None, :]
        gk = jnp.tile(qk_gain_k[l].astype(F32), 2)[None, :]
        swq, gaq_pad, swk, gak, gavt = _prep(p, rope_tabs, gq, gk, 1024, True, True, GA_TK)
        swq_c, gaq_c, swk_c, gak_c, gavt_c = _prep(pc, rope_tabs, gq, gk, lc, False, False, lc)

        sink_tab = jnp.broadcast_to(swa_sink[l].astype(F32)[:, None], (8, 128))
        o_a = _na_attention(p, pc, _na_bias_table(na_rpb[l]))
        o_b = _swa_attention(swq, swk, p, swk_c, pc, sink_tab)
        o_c = _ga_attention(gaq_pad, gak, gavt, gak_c, gavt_c)

        w8 = _pad_rows(conv_w[l].astype(F32), 8)
        cb = conv_b[l].astype(F32)[None, :]
        u = _conv(p, w8, cb, 512)
        uc = _conv(pc, w8, cb, lc)
        dtb = jnp.concatenate([pad8(dt_bias[l, 0].astype(F32)), pad8(dt_bias[l, 1].astype(F32))])[None, :]
        a_full = jnp.concatenate([pad8(a_neg[l, 0]), pad8(a_neg[l, 1])])[None, :]
        h_zero = jnp.zeros((bsz, 2, 2, SSM_STATE, 256), F32)
        yc, h_ctx = _ssd(uc, dtc_raw, dtb, a_full, h_zero, lc)
        y, _ = _ssd(u, dt_raw, dtb, a_full, h_ctx, 256)
        dsk = jnp.repeat(d_skip[l].astype(F32), HEAD_DIM)[None, :]
        ng = ssm_norm_g[l].astype(F32)[None, :]
        o_d = _ssd_finish(y, u, p, dsk, ng, 1024)

        is_moe = l % 2 == 1
        i = l // 2
        router = None
        if is_moe:
            wr = jnp.concatenate([moe_w_router[i].astype(F32), jnp.zeros((d, 128 - N_EXPERTS), F32)], axis=-1)
            br = jnp.concatenate([moe_b_router[i].astype(F32), jnp.full((128 - N_EXPERTS,), NEG_INF, F32)])[None, :]
            router = (wr, br)
        lng1, lnb1 = ln1_g[l].astype(F32)[None, :], ln1_b[l].astype(F32)[None, :]
        lng2, lnb2 = ln2_g[l].astype(F32)[None, :], ln2_b[l].astype(F32)[None, :]
        bg = b_gate[l].astype(F32)[:, None, :]
        res = _merge(x, mod_lat, (o_a, o_b, o_c, o_d), w_gate_b[l], bg, w_branch_b[l], w_out_b[l], lng1, lnb1,
                     alpha, 512, router)
        if with_ctx:
            o_ac = _ctx_attention(pc, C512_NAQ, pc, C512_NAK, pc, C512_NAV, 8, sink_tab, True, False)
            o_bc = _ctx_attention(swq_c, 0, swk_c, 0, pc, C128_SWV, 2, sink_tab, False, True)
            o_cc = _ctx_attention(gaq_c, 0, gak_c, 0, pc, C128_GAV, 2, sink_tab, False, False)
            o_dc = _ssd_finish(yc, uc, pc, dsk, ng, lc)
            to2 = lambda t: t.reshape(1, n_ctx, BRANCH_W)
            res_c = _merge(ctx2, mod_ctx, (to2(o_ac), to2(o_bc), to2(o_cc), to2(o_dc)), w_gate_b[l], bg,
                           w_branch_b[l], w_out_b[l], lng1, lnb1, alpha, min(512, n_ctx), router)

        if not is_moe:
            x = _ffn(res[0], mod_lat, ffn_up_b[i], ffn_down_b[i], lng2, lnb2, alpha, 1024)
            if with_ctx:
                ctx2 = _ffn(res_c[0], mod_ctx, ffn_up_b[i], ffn_down_b[i], lng2, lnb2, alpha, n_ctx)
        else:
            x1, h2, lg = res
            tok = h2.reshape(bsz * seq, d)
            lgs = lg.reshape(bsz * seq, 128)
            if with_ctx:
                x1c, h2c, lgc = res_c
                tok = jnp.concatenate([tok, h2c.reshape(n_ctx, d)], axis=0)
                lgs = jnp.concatenate([lgs, lgc.reshape(n_ctx, 128)], axis=0)
            info, cnt = _route(lgs)
            dest, blk_expert, n_valid, n_pad = _moe_plan(info, cnt[0])
            x_pad = _scatter_rows(dest, tok, n_pad)
            y_pad = _experts(x_pad, blk_expert, n_valid, moe_up_b[i], moe_down_b[i])
            n_lat = bsz * seq
            x = _combine(dest[:n_lat], info[:n_lat], x1, mod_lat, lng2, lnb2, y_pad, alpha)
            if with_ctx:
                ctx2 = _combine(dest[n_lat:], info[n_lat:], x1c, mod_ctx, lng2, lnb2, y_pad, alpha)
    return x
```

```python
import functools
import math

import jax
import jax.numpy as jnp
from jax import lax
from jax.experimental import pallas as pl
from jax.experimental.pallas import tpu as pltpu

F32 = jnp.float32
BF16 = jnp.bfloat16
HI = lax.Precision.HIGHEST

D_MODEL = 1024
GRID_W = 64
HEAD_DIM = 64
ATTN_SCALE = HEAD_DIM ** -0.5
ROPE_THETA = 10000.0
NEG_INF = -1e30
NA_HEADS = 8
NA_KH = 8
NA_KW = 16
SWA_WINDOW = 128
SSM_HEADS = 8
SSM_INNER = 512
SSM_STATE = 128
SSM_CONV = 5
N_BRANCH = 4
BRANCH_W = 512
D_FF = 2816
N_EXPERTS = 8
D_FF_EXPERT = 3584
MOE_BLOCK = 512
LN_EPS = 1e-5
RMS_EPS = 1e-6

P_COLS = 4608
C512_NAQ, C512_NAK, C512_NAV, C512_SWQ, C512_GAQ, C512_Z = 0, 1, 2, 3, 4, 5
C1024_XBC = 3
C128_SWK, C128_SWV, C128_GAK, C128_GAV = 32, 33, 34, 35
DT_COLS = 256

VMEM_LIMIT = 48 * 1024 * 1024


def _cp(*sem):
    return pltpu.CompilerParams(dimension_semantics=sem, vmem_limit_bytes=VMEM_LIMIT)


def _sigmoid(x):
    return 1.0 / (1.0 + jnp.exp(-x))


def _silu(x):
    return x * _sigmoid(x)


def _softplus(x):
    return jnp.maximum(x, 0.0) + jnp.log1p(jnp.exp(-jnp.abs(x)))


def _layer_norm(y, g, b):
    mu = jnp.mean(y, axis=-1, keepdims=True)
    yc = y - mu
    var = jnp.mean(yc * yc, axis=-1, keepdims=True)
    return yc * lax.rsqrt(var + LN_EPS) * g + b


def _dot(a, b):
    return jnp.dot(a, b, preferred_element_type=F32)


def _dot_nt(a, b):
    return lax.dot_general(a, b, (((1,), (1,)), ((), ())), preferred_element_type=F32)


def _mod_kernel(s_ref, w_ref, b_ref, o_ref):
    s = _silu(s_ref[...])
    o_ref[0] = jnp.dot(s, w_ref[0], precision=HI, preferred_element_type=F32) + b_ref[0]


def _modulation(s_rows, w_mod, b_mod):
    n_l, d, d6 = w_mod.shape
    tn = 1024
    return pl.pallas_call(
        _mod_kernel,
        grid=(n_l, d6 // tn),
        in_specs=[pl.BlockSpec((8, d), lambda l, j: (0, 0)),
                  pl.BlockSpec((1, d, tn), lambda l, j: (l, 0, j)),
                  pl.BlockSpec((1, 1, tn), lambda l, j: (l, 0, j))],
        out_specs=pl.BlockSpec((1, 8, tn), lambda l, j: (l, 0, j)),
        out_shape=jax.ShapeDtypeStruct((n_l, 8, d6), F32),
        compiler_params=_cp("parallel", "parallel"),
        name="modulation",
    )(s_rows, w_mod, b_mod.reshape(n_l, 1, d6))


def _inproj_kernel(x_ref, mod_ref, w_ref, wdt_ref, p_ref, dt_ref, h_scr):
    @pl.when(pl.program_id(2) == 0)
    def _():
        m = mod_ref[0]
        h = (x_ref[0] * (1.0 + m[1:2]) + m[0:1]).astype(BF16)
        h_scr[...] = h
        dt_ref[0] = _dot(h, wdt_ref[...])

    p_ref[0] = _dot(h_scr[...], w_ref[...]).astype(BF16)


def _inproj(x3, mod, w_main, w_dt, tm):
    g, r, d = x3.shape
    tn = 1152
    return pl.pallas_call(
        _inproj_kernel,
        grid=(g, r // tm, P_COLS // tn),
        in_specs=[pl.BlockSpec((1, tm, d), lambda b, i, j: (b, i, 0)),
                  pl.BlockSpec((1, 8, d), lambda b, i, j: (b, 0, 0)),
                  pl.BlockSpec((d, tn), lambda b, i, j: (0, j)),
                  pl.BlockSpec((d, DT_COLS), lambda b, i, j: (0, 0))],
        out_specs=[pl.BlockSpec((1, tm, tn), lambda b, i, j: (b, i, j)),
                   pl.BlockSpec((1, tm, DT_COLS), lambda b, i, j: (b, i, 0))],
        out_shape=[jax.ShapeDtypeStruct((g, r, P_COLS), BF16),
                   jax.ShapeDtypeStruct((g, r, DT_COLS), F32)],
        scratch_shapes=[pltpu.VMEM((tm, d), BF16)],
        compiler_params=_cp("parallel", "parallel", "arbitrary"),
        name="inproj",
    )(x3, mod, w_main, w_dt)


GA_TK = 512
LOG2E = math.log2(math.e)


def _prep_kernel(swq_ref, gaq_ref, swk_ref, gak_ref, gav_ref, cos_ref, sa_ref, sb_ref, gq_ref, gk_ref,
                 oswq, ogaq, oswk, ogak, ogavt, *, rope, pad_q, vt_chunk):
    row = lax.broadcasted_iota(jnp.int32, (128, 128), 0) // HEAD_DIM
    col = lax.broadcasted_iota(jnp.int32, (128, 128), 1) // HEAD_DIM
    head_mean = jnp.where(row == col, 1.0 / HEAD_DIM, 0.0).astype(F32)
    first_half = lax.broadcasted_iota(jnp.int32, (1, 128), 1) < HEAD_DIM

    def rot(x):
        if not rope:
            return x
        return (x * cos_ref[...] + pltpu.roll(x, 128 - HEAD_DIM // 2, 1) * sa_ref[...]
                + pltpu.roll(x, HEAD_DIM // 2, 1) * sb_ref[...])

    def rms(x, g):
        ms = jnp.dot(x * x, head_mean, precision=HI, preferred_element_type=F32)
        return x * lax.rsqrt(ms + RMS_EPS) * g

    def put_padded(o_ref, x, c):
        swapped = pltpu.roll(x, HEAD_DIM, 1)
        keep = first_half if c < 2 else jnp.logical_not(first_half)
        even, odd = (x, swapped) if c < 2 else (swapped, x)
        o_ref[0, :, 2 * c * 128:(2 * c + 1) * 128] = jnp.where(keep, even, 0.0).astype(BF16)
        o_ref[0, :, (2 * c + 1) * 128:(2 * c + 2) * 128] = jnp.where(keep, odd, 0.0).astype(BF16)

    for c in range(4):
        sl = slice(c * 128, (c + 1) * 128)
        sq_c = rot(swq_ref[0, :, sl].astype(F32)) * ATTN_SCALE
        gq_c = rot(rms(gaq_ref[0, :, sl].astype(F32), gq_ref[...]))
        if pad_q:
            put_padded(oswq, sq_c, c)
            put_padded(ogaq, gq_c * (ATTN_SCALE * LOG2E), c)
        else:
            oswq[0, :, sl] = sq_c.astype(BF16)
            ogaq[0, :, sl] = (gq_c * ATTN_SCALE).astype(BF16)
    oswk[0] = rot(swk_ref[0].astype(F32)).astype(BF16)
    ogak[0] = rot(rms(gak_ref[0].astype(F32), gk_ref[...])).astype(BF16)
    for cc in range(gav_ref.shape[1] // vt_chunk):
        ogavt[0, cc] = gav_ref[0, cc * vt_chunk:(cc + 1) * vt_chunk, :].astype(F32).T.astype(BF16)


def _prep(p, tables, gq, gk, tm, rope, pad_q, vt_chunk):
    g, r, _ = p.shape
    cosf, sin_a, sin_b = tables
    tab = pl.BlockSpec((tm, 128), lambda b, i: (i, 0))
    vec = pl.BlockSpec((1, 128), lambda b, i: (0, 0))
    qw = 1024 if pad_q else 512
    n_vt = tm // vt_chunk
    return pl.pallas_call(
        functools.partial(_prep_kernel, rope=rope, pad_q=pad_q, vt_chunk=vt_chunk),
        grid=(g, r // tm),
        in_specs=[pl.BlockSpec((1, tm, 512), lambda b, i: (b, i, C512_SWQ)),
                  pl.BlockSpec((1, tm, 512), lambda b, i: (b, i, C512_GAQ)),
                  pl.BlockSpec((1, tm, 128), lambda b, i: (b, i, C128_SWK)),
                  pl.BlockSpec((1, tm, 128), lambda b, i: (b, i, C128_GAK)),
                  pl.BlockSpec((1, tm, 128), lambda b, i: (b, i, C128_GAV)),
                  tab, tab, tab, vec, vec],
        out_specs=[pl.BlockSpec((1, tm, qw), lambda b, i: (b, i, 0)),
                   pl.BlockSpec((1, tm, qw), lambda b, i: (b, i, 0)),
                   pl.BlockSpec((1, tm, 128), lambda b, i: (b, i, 0)),
                   pl.BlockSpec((1, tm, 128), lambda b, i: (b, i, 0)),
                   pl.BlockSpec((1, n_vt, 128, vt_chunk), lambda b, i: (b, i, 0, 0))],
        out_shape=[jax.ShapeDtypeStruct((g, r, qw), BF16), jax.ShapeDtypeStruct((g, r, qw), BF16),
                   jax.ShapeDtypeStruct((g, r, 128), BF16), jax.ShapeDtypeStruct((g, r, 128), BF16),
                   jax.ShapeDtypeStruct((g, r // vt_chunk, 128, vt_chunk), BF16)],
        compiler_params=_cp("parallel", "parallel"),
        name="qk_prep",
    )(p, p, p, p, p, cosf, sin_a, sin_b, gq, gk)


def _rope_tables(n_tok):
    t = jnp.arange(n_tok, dtype=jnp.int32)
    row = (t // GRID_W).astype(F32)
    col = (t % GRID_W).astype(F32)
    n_freq = HEAD_DIM // 4
    inv_freq = ROPE_THETA ** (-jnp.arange(n_freq, dtype=F32) / n_freq)
    ang = jnp.concatenate([row[:, None] * inv_freq, col[:, None] * inv_freq], axis=-1)
    cos, sin = jnp.cos(ang), jnp.sin(ang)
    zero = jnp.zeros_like(sin)
    cosf = jnp.tile(jnp.concatenate([cos, cos], -1), (1, 2))
    sin_a = jnp.tile(jnp.concatenate([-sin, zero], -1), (1, 2))
    sin_b = jnp.tile(jnp.concatenate([zero, sin], -1), (1, 2))
    return cosf, sin_a, sin_b


def _conv_kernel(prev_ref, cur_ref, next_ref, w_ref, b_ref, o_ref, *, tc):
    i = pl.program_id(1)
    n = pl.num_programs(1)
    prev = jnp.where(i > 0, prev_ref[0].astype(F32), 0.0)
    nxt = jnp.where(i < n - 1, next_ref[0].astype(F32), 0.0)
    ext = jnp.concatenate([prev, cur_ref[0].astype(F32), nxt], axis=0)
    rows = tc + 32
    acc = b_ref[...] + jnp.zeros((tc, ext.shape[1]), F32)
    for j in range(SSM_CONV):
        shift = (SSM_CONV // 2 - j) % rows
        sh = ext if shift == 0 else pltpu.roll(ext, shift, 0)
        acc = acc + sh[16:16 + tc] * w_ref[j:j + 1, :]
    o_ref[0] = _silu(acc).astype(BF16)


def _conv(p, w8, b, tc):
    g, r, _ = p.shape
    nb16 = r // 16
    k = tc // 16
    return pl.pallas_call(
        functools.partial(_conv_kernel, tc=tc),
        grid=(g, r // tc),
        in_specs=[pl.BlockSpec((1, 16, 1024), lambda b_, i: (b_, jnp.maximum(i * k - 1, 0), C1024_XBC)),
                  pl.BlockSpec((1, tc, 1024), lambda b_, i: (b_, i, C1024_XBC)),
                  pl.BlockSpec((1, 16, 1024), lambda b_, i: (b_, jnp.minimum((i + 1) * k, nb16 - 1), C1024_XBC)),
                  pl.BlockSpec((8, 1024), lambda b_, i: (0, 0)),
                  pl.BlockSpec((1, 1024), lambda b_, i: (0, 0))],
        out_specs=pl.BlockSpec((1, tc, 1024), lambda b_, i: (b_, i, 0)),
        out_shape=jax.ShapeDtypeStruct((g, r, 1024), BF16),
        compiler_params=_cp("parallel", "parallel"),
        name="dwconv_silu",
    )(p, p, p, w8, b)


def _softmax_out(units):
    probs = []
    for parts, sink in units:
        m = parts[0][0].max(axis=-1, keepdims=True)
        for s, _ in parts[1:]:
            m = jnp.maximum(m, s.max(axis=-1, keepdims=True))
        if sink is not None:
            m = jnp.maximum(m, sink)
        l = jnp.exp(sink - m) if sink is not None else 0.0
        es = []
        for s, _ in parts:
            e = jnp.exp(s - m)
            l = l + e.sum(axis=-1, keepdims=True)
            es.append(e.astype(BF16))
        probs.append((es, l))
    outs = []
    for (parts, _), (es, l) in zip(units, probs):
        acc = _dot(es[0], parts[0][1])
        for e, (_, v) in zip(es[1:], parts[1:]):
            acc = acc + _dot(e, v)
        outs.append(acc / l)
    return outs


NA_ROWS_PER_STEP = 4


def _na_kernel(q_ref, k_ref, v_ref, kc_ref, vc_ref, bias_ref, o_ref, *, n_rows):
    blk = pl.program_id(2)
    kc = kc_ref[0]
    vc = vc_ref[0]
    first = lax.broadcasted_iota(jnp.int32, (1, 128), 1) < HEAD_DIM
    keeps = (first, jnp.logical_not(first))
    vcs = [jnp.where(keep, vc, 0) for keep in keeps]

    units = []
    for rr in range(NA_ROWS_PER_STEP):
        r = blk * NA_ROWS_PER_STEP + rr
        start = jnp.clip(r - NA_KH // 2, 0, n_rows - NA_KH)
        off = start - r + NA_KH - 1
        q = q_ref[0, rr * GRID_W:(rr + 1) * GRID_W, :] * ATTN_SCALE
        k0 = pl.multiple_of(start * GRID_W, GRID_W)
        kw = k_ref[0, pl.ds(k0, NA_KH * GRID_W), :]
        vw = v_ref[0, pl.ds(k0, NA_KH * GRID_W), :]
        for hh in range(2):
            qh = jnp.where(keeps[hh], q, 0)
            s_loc = _dot_nt(qh, kw) + bias_ref[off, hh]
            s_ctx = _dot_nt(qh, kc)
            units.append(([(s_loc, jnp.where(keeps[hh], vw, 0)), (s_ctx, vcs[hh])], None))
    outs = _softmax_out(units)
    for rr in range(NA_ROWS_PER_STEP):
        o_ref[0, rr * GRID_W:(rr + 1) * GRID_W, :] = (outs[2 * rr] + outs[2 * rr + 1]).astype(BF16)


def _na_bias_table(rpb):
    o = jnp.arange(NA_KH)
    kr = jnp.arange(NA_KH)
    dr = o[:, None] + kr[None, :]
    c = jnp.arange(GRID_W)
    kcol = jnp.arange(GRID_W)
    cs = jnp.clip(c - NA_KW // 2, 0, GRID_W - NA_KW)
    dc = kcol[None, :] - c[:, None] + NA_KW - 1
    valid = (kcol[None, :] >= cs[:, None]) & (kcol[None, :] < cs[:, None] + NA_KW)
    tbl = rpb[:, dr][:, :, :, jnp.clip(dc, 0, 2 * NA_KW - 2)]
    tbl = jnp.where(valid[None, None, None], tbl.astype(F32), NEG_INF)
    return tbl.transpose(1, 0, 3, 2, 4).reshape(NA_KH, NA_HEADS, GRID_W, NA_KH * GRID_W)


def _na_attention(p, pc, bias):
    b, s, _ = p.shape
    lc = pc.shape[1]
    n_rows = s // GRID_W
    tq = NA_ROWS_PER_STEP * GRID_W
    return pl.pallas_call(
        functools.partial(_na_kernel, n_rows=n_rows),
        grid=(b, NA_HEADS // 2, s // tq),
        in_specs=[pl.BlockSpec((1, tq, 128), lambda b_, h, i: (b_, i, 4 * C512_NAQ + h)),
                  pl.BlockSpec((1, s, 128), lambda b_, h, i: (b_, 0, 4 * C512_NAK + h)),
                  pl.BlockSpec((1, s, 128), lambda b_, h, i: (b_, 0, 4 * C512_NAV + h)),
                  pl.BlockSpec((1, lc, 128), lambda b_, h, i: (b_, 0, 4 * C512_NAK + h)),
                  pl.BlockSpec((1, lc, 128), lambda b_, h, i: (b_, 0, 4 * C512_NAV + h)),
                  pl.BlockSpec((NA_KH, 2, GRID_W, NA_KH * GRID_W), lambda b_, h, i: (0, h, 0, 0))],
        out_specs=pl.BlockSpec((1, tq, 128), lambda b_, h, i: (b_, i, h)),
        out_shape=jax.ShapeDtypeStruct((b, s, 512), BF16),
        compiler_params=_cp("parallel", "parallel", "arbitrary"),
        name="na_attention",
    )(p, p, p, pc, pc, bias)


SWA_TQ = 256
SWA_BAND = SWA_TQ + 2 * SWA_WINDOW


def _swa_kernel(q_ref, k_ref, v_ref, kc_ref, vc_ref, sink_ref, o_ref, *, n_tok):
    n = pl.program_id(1)
    start = pl.multiple_of(jnp.clip(n * SWA_TQ - SWA_WINDOW, 0, n_tok - SWA_BAND), SWA_WINDOW)
    kb = k_ref[0, pl.ds(start, SWA_BAND), :]
    vb = v_ref[0, pl.ds(start, SWA_BAND), :]
    kc = kc_ref[0]
    vc = vc_ref[0]
    kpos = start + lax.broadcasted_iota(jnp.int32, (SWA_BAND, SWA_TQ), 0)
    qpos = n * SWA_TQ + lax.broadcasted_iota(jnp.int32, (SWA_BAND, SWA_TQ), 1)
    in_win = jnp.abs(kpos - qpos) <= SWA_WINDOW
    vbt = vb.astype(F32).T.astype(BF16)
    vct = vc.astype(F32).T.astype(BF16)
    scores = []
    for h in range(8):
        qh = q_ref[0, :, h * 128:(h + 1) * 128]
        scores.append((jnp.where(in_win, _dot_nt(kb, qh), NEG_INF), _dot_nt(kc, qh)))
    probs = []
    for h, (s_loc, s_ctx) in enumerate(scores):
        sink = sink_ref[h:h + 1, 0:1]
        m = jnp.maximum(jnp.maximum(s_loc.max(axis=0, keepdims=True), s_ctx.max(axis=0, keepdims=True)), sink)
        e_loc = jnp.exp(s_loc - m)
        e_ctx = jnp.exp(s_ctx - m)
        l = jnp.exp(sink - m) + e_loc.sum(axis=0, keepdims=True) + e_ctx.sum(axis=0, keepdims=True)
        probs.append((e_loc.astype(BF16), e_ctx.astype(BF16), l))
    outs = []
    for h, (e_loc, e_ctx, l) in enumerate(probs):
        rows = slice((h // 4) * HEAD_DIM, (h // 4 + 1) * HEAD_DIM)
        outs.append((_dot(vbt[rows, :], e_loc) + _dot(vct[rows, :], e_ctx)) / l)
    o_ref[0] = jnp.concatenate(outs, axis=0).T.astype(BF16)


def _swa_attention(q, k, p, kc, pc, sink_tab):
    b, s, _ = k.shape
    lc = kc.shape[1]
    return pl.pallas_call(
        functools.partial(_swa_kernel, n_tok=s),
        grid=(b, s // SWA_TQ),
        in_specs=[pl.BlockSpec((1, SWA_TQ, 1024), lambda b_, i: (b_, i, 0)),
                  pl.BlockSpec((1, s, 128), lambda b_, i: (b_, 0, 0)),
                  pl.BlockSpec((1, s, 128), lambda b_, i: (b_, 0, C128_SWV)),
                  pl.BlockSpec((1, lc, 128), lambda b_, i: (b_, 0, 0)),
                  pl.BlockSpec((1, lc, 128), lambda b_, i: (b_, 0, C128_SWV)),
                  pl.BlockSpec((8, 128), lambda b_, i: (0, 0))],
        out_specs=pl.BlockSpec((1, SWA_TQ, 512), lambda b_, i: (b_, i, 0)),
        out_shape=jax.ShapeDtypeStruct((b, s, 512), BF16),
        compiler_params=_cp("parallel", "arbitrary"),
        name="swa_attention",
    )(q, k, p, kc, pc, sink_tab)


GA_TQ = 256


def _ga_kernel(q_ref, k_ref, vt_ref, kc_ref, vct_ref, o_ref, acc_scr, *, n_chunks):
    acc_scr[...] = jnp.zeros_like(acc_scr)

    def step(carry, kk, vt):
        m_all, l_all = carry

        def scores(h):
            return _dot_nt(kk, q_ref[0, :, h * 128:(h + 1) * 128])

        m_rows, l_rows = [], []
        sts = [scores(h) for h in range(8)]
        for h in range(8):
            st = sts[h]
            g = h // 4
            m_old = m_all[h:h + 1, :]
            m_new = jnp.maximum(m_old, st.max(axis=0, keepdims=True))
            a = jnp.exp2(m_old - m_new)
            e = jnp.exp2(st - m_new)
            l_rows.append(a * l_all[h:h + 1, :] + e.sum(axis=0, keepdims=True))
            m_rows.append(m_new)
            rows = slice(h * HEAD_DIM, (h + 1) * HEAD_DIM)
            acc_scr[rows, :] = a * acc_scr[rows, :] + _dot(vt[g * HEAD_DIM:(g + 1) * HEAD_DIM, :], e.astype(BF16))
        return jnp.concatenate(m_rows, axis=0), jnp.concatenate(l_rows, axis=0)

    def chunk(c, carry):
        k0 = pl.multiple_of(c * GA_TK, GA_TK)
        return step(carry, k_ref[0, pl.ds(k0, GA_TK), :], vt_ref[0, c])

    init = (jnp.full((8, GA_TQ), NEG_INF, F32), jnp.zeros((8, GA_TQ), F32))
    carry = lax.fori_loop(0, n_chunks, chunk, init)
    _, l_fin = step(carry, kc_ref[0], vct_ref[0, 0])
    inv = 1.0 / l_fin
    out_t = jnp.concatenate([acc_scr[h * HEAD_DIM:(h + 1) * HEAD_DIM, :] * inv[h:h + 1, :] for h in range(8)], axis=0)
    o_ref[0] = out_t.T.astype(BF16)


def _ga_attention(q_pad, k, vt, kc, vct):
    b, s, _ = k.shape
    lc = kc.shape[1]
    n_chunks = s // GA_TK
    return pl.pallas_call(
        functools.partial(_ga_kernel, n_chunks=n_chunks),
        grid=(b, s // GA_TQ),
        in_specs=[pl.BlockSpec((1, GA_TQ, 1024), lambda b_, i: (b_, i, 0)),
                  pl.BlockSpec((1, s, 128), lambda b_, i: (b_, 0, 0)),
                  pl.BlockSpec((1, n_chunks, 128, GA_TK), lambda b_, i: (b_, 0, 0, 0)),
                  pl.BlockSpec((1, lc, 128), lambda b_, i: (b_, 0, 0)),
                  pl.BlockSpec((1, 1, 128, lc), lambda b_, i: (b_, 0, 0, 0))],
        out_specs=pl.BlockSpec((1, GA_TQ, 512), lambda b_, i: (b_, i, 0)),
        out_shape=jax.ShapeDtypeStruct((b, s, 512), BF16),
        scratch_shapes=[pltpu.VMEM((8 * HEAD_DIM, GA_TQ), F32)],
        compiler_params=_cp("parallel", "arbitrary"),
        name="ga_attention",
    )(q_pad, k, vt, kc, vct)


def _ctx_attn_kernel(q_ref, k_ref, v_ref, sink_ref, o_ref, *, n_kv, q_scale, use_sink):
    rep = 8 // n_kv
    units = []
    for h in range(8):
        g = h // rep
        gs = slice(g * HEAD_DIM, (g + 1) * HEAD_DIM)
        qh = q_ref[0, :, h * HEAD_DIM:(h + 1) * HEAD_DIM]
        if q_scale:
            qh = qh * ATTN_SCALE
        s = _dot_nt(qh, k_ref[0, :, gs])
        units.append(([(s, v_ref[0, :, gs])], sink_ref[h:h + 1, 0:1] if use_sink else None))
    o_ref[0] = jnp.concatenate(_softmax_out(units), axis=-1).astype(BF16)


def _ctx_attention(q, qcol, k, kcol, v, vcol, n_kv, sink_tab, q_scale, use_sink):
    b, lc, _ = q.shape
    kvw = n_kv * HEAD_DIM
    return pl.pallas_call(
        functools.partial(_ctx_attn_kernel, n_kv=n_kv, q_scale=q_scale, use_sink=use_sink),
        grid=(b,),
        in_specs=[pl.BlockSpec((1, lc, 512), lambda b_: (b_, 0, qcol)),
                  pl.BlockSpec((1, lc, kvw), lambda b_: (b_, 0, kcol)),
                  pl.BlockSpec((1, lc, kvw), lambda b_: (b_, 0, vcol)),
                  pl.BlockSpec((8, 128), lambda b_: (0, 0))],
        out_specs=pl.BlockSpec((1, lc, 512), lambda b_: (b_, 0, 0)),
        out_shape=jax.ShapeDtypeStruct((b, lc, 512), BF16),
        compiler_params=_cp("parallel"),
        name="ctx_attention",
    )(q, k, v, sink_tab)


def _ssd_kernel(u_ref, dt_ref, dtb_ref, a_ref, h0_ref, y_ref, hT_ref, st, *, q_len):
    d = pl.program_id(1)
    c = pl.program_id(2)

    @pl.when(c == 0)
    def _():
        st[...] = h0_ref[0, 0]

    u = u_ref[0]
    dt = _softplus(dt_ref[0] + dtb_ref[...])
    da = dt * a_ref[...]
    row = lax.broadcasted_iota(jnp.int32, (q_len, q_len), 0)
    col = lax.broadcasted_iota(jnp.int32, (q_len, q_len), 1)
    causal = (row - col) * (1 - 2 * d) >= 0
    acum = jnp.dot(causal.astype(F32), da, precision=HI, preferred_element_type=F32)
    acum_t = acum.T
    total = jnp.sum(da, axis=0, keepdims=True)
    cbs, bg_ts, s_gs, y_offs = [], [], [], []
    for g in range(2):
        bg = u[:, SSM_INNER + g * SSM_STATE:SSM_INNER + (g + 1) * SSM_STATE]
        cg = u[:, SSM_INNER + (2 + g) * SSM_STATE:SSM_INNER + (3 + g) * SSM_STATE]
        cbs.append(_dot_nt(cg, bg))
        bg_ts.append(bg.astype(F32).T.astype(BF16))
        s_gs.append(st[g])
        y_offs.append(_dot(cg, s_gs[g].astype(BF16)))
    first = lax.broadcasted_iota(jnp.int32, (1, 128), 1) < HEAD_DIM
    mixes, xms, xdecs, e_acs, e_tots = [], [], [], [], []
    for pr in range(SSM_HEADS // 2):
        h0, h1 = 2 * pr, 2 * pr + 1
        ac0, ac1 = acum[:, h0:h0 + 1], acum[:, h1:h1 + 1]
        ac_pair = jnp.where(first, ac0, ac1)
        tot_pair = jnp.where(first, total[:, h0:h0 + 1], total[:, h1:h1 + 1])
        xdt = u[:, pr * 128:(pr + 1) * 128].astype(F32) * jnp.where(first, dt[:, h0:h0 + 1], dt[:, h1:h1 + 1])
        for h, ac, keep in ((h0, ac0, first), (h1, ac1, jnp.logical_not(first))):
            seg = jnp.exp(jnp.where(causal, ac - acum_t[h:h + 1, :], NEG_INF))
            mixes.append((cbs[pr // 2] * seg).astype(BF16))
            xms.append(jnp.where(keep, xdt, 0.0).astype(BF16))
        xdecs.append((xdt * jnp.exp(tot_pair - ac_pair)).astype(BF16))
        e_acs.append(jnp.exp(ac_pair))
        e_tots.append(jnp.exp(tot_pair))
    y_diag = [_dot(mixes[2 * pr], xms[2 * pr]) + _dot(mixes[2 * pr + 1], xms[2 * pr + 1])
              for pr in range(SSM_HEADS // 2)]
    upd = [_dot(bg_ts[pr // 2], xdecs[pr]) for pr in range(SSM_HEADS // 2)]
    ys = []
    for g in range(2):
        new_cols = []
        for pp in range(2):
            pr = 2 * g + pp
            ls = slice(pp * 128, (pp + 1) * 128)
            ys.append(y_diag[pr] + y_offs[g][:, ls] * e_acs[pr])
            new_cols.append(s_gs[g][:, ls] * e_tots[pr] + upd[pr])
        st[g] = jnp.concatenate(new_cols, axis=-1)
    y_ref[0, 0] = jnp.concatenate(ys, axis=-1)

    @pl.when(c == pl.num_programs(2) - 1)
    def _():
        hT_ref[0, 0] = st[...]


def _ssd(u, dt_raw, dtb, a_full, h0, q_len):
    b, l, _ = u.shape
    n_c = l // q_len

    def cidx(d, c):
        return c + d * (n_c - 1 - 2 * c)

    return pl.pallas_call(
        functools.partial(_ssd_kernel, q_len=q_len),
        grid=(b, 2, n_c),
        in_specs=[pl.BlockSpec((1, q_len, 1024), lambda b_, d, c: (b_, cidx(d, c), 0)),
                  pl.BlockSpec((1, q_len, 128), lambda b_, d, c: (b_, cidx(d, c), d)),
                  pl.BlockSpec((1, 128), lambda b_, d, c: (0, d)),
                  pl.BlockSpec((1, 128), lambda b_, d, c: (0, d)),
                  pl.BlockSpec((1, 1, 2, SSM_STATE, 256), lambda b_, d, c: (b_, d, 0, 0, 0))],
        out_specs=[pl.BlockSpec((1, 1, q_len, 512), lambda b_, d, c: (d, b_, cidx(d, c), 0)),
                   pl.BlockSpec((1, 1, 2, SSM_STATE, 256), lambda b_, d, c: (b_, d, 0, 0, 0))],
        out_shape=[jax.ShapeDtypeStruct((2, b, l, 512), F32),
                   jax.ShapeDtypeStruct((b, 2, 2, SSM_STATE, 256), F32)],
        scratch_shapes=[pltpu.VMEM((2, SSM_STATE, 256), F32)],
        compiler_params=_cp("parallel", "arbitrary", "arbitrary"),
        name="ssd_scan",
    )(u, dt_raw, dtb, a_full, h0)


def _ssd_finish_kernel(yf_ref, yb_ref, xs_ref, z_ref, dsk_ref, g_ref, o_ref):
    y = yf_ref[0, 0] + yb_ref[0, 0] + dsk_ref[...] * xs_ref[0].astype(F32)
    y = y * _silu(z_ref[0].astype(F32))
    ms = jnp.mean(y * y, axis=-1, keepdims=True)
    o_ref[0] = (y * lax.rsqrt(ms + RMS_EPS) * g_ref[...]).astype(BF16)


def _ssd_finish(y, u, p, dsk, g, tm):
    _, b, l, _ = y.shape
    vec = pl.BlockSpec((1, 512), lambda b_, i: (0, 0))
    return pl.pallas_call(
        _ssd_finish_kernel,
        grid=(b, l // tm),
        in_specs=[pl.BlockSpec((1, 1, tm, 512), lambda b_, i: (0, b_, i, 0)),
                  pl.BlockSpec((1, 1, tm, 512), lambda b_, i: (1, b_, i, 0)),
                  pl.BlockSpec((1, tm, 512), lambda b_, i: (b_, i, 0)),
                  pl.BlockSpec((1, tm, 512), lambda b_, i: (b_, i, C512_Z)),
                  vec, vec],
        out_specs=pl.BlockSpec((1, tm, 512), lambda b_, i: (b_, i, 0)),
        out_shape=jax.ShapeDtypeStruct((b, l, 512), BF16),
        compiler_params=_cp("parallel", "parallel"),
        name="ssd_finish",
    )(y, y, u, p, dsk, g)


def _merge_kernel(*refs, alpha, moe):
    if moe:
        (x_ref, mod_ref, oa, ob, oc, od, wg_ref, bg_ref, wb_ref, wo_ref, lng_ref, lnb_ref, wr_ref, br_ref,
         x1_ref, h2_ref, lg_ref) = refs
    else:
        (x_ref, mod_ref, oa, ob, oc, od, wg_ref, bg_ref, wb_ref, wo_ref, lng_ref, lnb_ref, x1_ref) = refs
    m = mod_ref[0]
    h = (x_ref[0] * (1.0 + m[1:2]) + m[0:1]).astype(BF16)
    acc = None
    for k, br in enumerate((oa, ob, oc, od)):
        term = _sigmoid(_dot(h, wg_ref[k]) + bg_ref[k]) * _dot(br[0], wb_ref[k])
        acc = term if acc is None else acc + term
    o = _dot(acc.astype(BF16), wo_ref[...])
    x1 = _layer_norm(alpha * x_ref[0] + m[2:3] * o, lng_ref[...], lnb_ref[...])
    x1_ref[0] = x1
    if moe:
        h2 = x1 * (1.0 + m[4:5]) + m[3:4]
        h2_ref[0] = h2
        h2_hi = h2.astype(BF16)
        h2_lo = (h2 - h2_hi.astype(F32)).astype(BF16)
        lg_ref[0] = (_dot(h2_hi, wr_ref[0]) + _dot(h2_lo, wr_ref[0]) + _dot(h2_hi, wr_ref[1])) + br_ref[...]


def _merge(x3, mod, branches, wg, bg, wb, wo, lng, lnb, alpha, tm, router=None):
    g, r, d = x3.shape
    moe = router is not None
    tile = lambda w: pl.BlockSpec((1, tm, w), lambda b, t: (b, t, 0))
    once = lambda shape: pl.BlockSpec(shape, lambda b, t: (0,) * len(shape), pipeline_mode=pl.Buffered(1))
    in_specs = [tile(d), pl.BlockSpec((1, 8, d), lambda b, t: (b, 0, 0)),
                tile(BRANCH_W), tile(BRANCH_W), tile(BRANCH_W), tile(BRANCH_W),
                once((N_BRANCH, d, d)), once((N_BRANCH, 1, d)), once((N_BRANCH, BRANCH_W, d)), once((d, d)),
                once((1, d)), once((1, d))]
    args = [x3, mod, *branches, wg, bg, wb, wo, lng, lnb]
    out_specs = [tile(d)]
    out_shape = [jax.ShapeDtypeStruct((g, r, d), F32)]
    if moe:
        in_specs += [once((2, d, 128)), once((1, 128))]
        args += list(router)
        out_specs += [tile(d), tile(128)]
        out_shape += [jax.ShapeDtypeStruct((g, r, d), F32), jax.ShapeDtypeStruct((g, r, 128), F32)]
    return pl.pallas_call(
        functools.partial(_merge_kernel, alpha=alpha, moe=moe),
        grid=(g, r // tm),
        in_specs=in_specs, out_specs=out_specs, out_shape=out_shape,
        compiler_params=_cp("parallel", "parallel"),
        name="merge_ln1",
    )(*args)


FFN_CHUNK = D_FF // 2


def _ffn_kernel(x_ref, mod_ref, wg_ref, wu_ref, wd_ref, lng_ref, lnb_ref, o_ref, h_scr, acc_scr, *, alpha):
    j = pl.program_id(2)
    m = mod_ref[0]

    @pl.when(j == 0)
    def _():
        h_scr[...] = (x_ref[0] * (1.0 + m[4:5]) + m[3:4]).astype(BF16)
        acc_scr[...] = jnp.zeros_like(acc_scr)

    h = h_scr[...]
    a = _silu(_dot(h, wg_ref[...])) * _dot(h, wu_ref[...])
    acc_scr[...] += _dot(a.astype(BF16), wd_ref[...])

    @pl.when(j == pl.num_programs(2) - 1)
    def _():
        o_ref[0] = _layer_norm(alpha * x_ref[0] + m[5:6] * acc_scr[...], lng_ref[...], lnb_ref[...])


def _ffn(x3, mod, w_up, w_down, lng, lnb, alpha, tm):
    g, r, d = x3.shape
    n_j = D_FF // FFN_CHUNK
    vec = pl.BlockSpec((1, d), lambda b, t, j: (0, 0))
    return pl.pallas_call(
        functools.partial(_ffn_kernel, alpha=alpha),
        grid=(g, r // tm, n_j),
        in_specs=[pl.BlockSpec((1, tm, d), lambda b, t, j: (b, t, 0)),
                  pl.BlockSpec((1, 8, d), lambda b, t, j: (b, 0, 0)),
                  pl.BlockSpec((d, FFN_CHUNK), lambda b, t, j: (0, j)),
                  pl.BlockSpec((d, FFN_CHUNK), lambda b, t, j: (0, n_j + j)),
                  pl.BlockSpec((FFN_CHUNK, d), lambda b, t, j: (j, 0)), vec, vec],
        out_specs=pl.BlockSpec((1, tm, d), lambda b, t, j: (b, t, 0)),
        out_shape=jax.ShapeDtypeStruct((g, r, d), F32),
        scratch_shapes=[pltpu.VMEM((tm, d), BF16), pltpu.VMEM((tm, d), F32)],
        compiler_params=_cp("parallel", "parallel", "arbitrary"),
        name="ffn_ln2",
    )(x3, mod, w_up, w_up, w_down, lng, lnb)


ROUTE_TM = 256


def _route_kernel(lg_ref, info_ref, cnt_ref, carry):
    i = pl.program_id(0)

    @pl.when(i == 0)
    def _():
        carry[...] = jnp.zeros_like(carry)

    lg = lg_ref[...]
    tm = lg.shape[0]
    lane = lax.broadcasted_iota(jnp.int32, lg.shape, 1).astype(F32)
    m1 = lg.max(axis=-1, keepdims=True)
    i1 = jnp.where(lg == m1, lane, 128.0).min(axis=-1, keepdims=True)
    lg2 = jnp.where(lane == i1, -jnp.inf, lg)
    m2 = lg2.max(axis=-1, keepdims=True)
    i2 = jnp.where(lg2 == m2, lane, 128.0).min(axis=-1, keepdims=True)
    e = jnp.exp(m2 - m1)
    w1 = 1.0 / (1.0 + e)
    w2 = e / (1.0 + e)
    sel = jnp.where((lane == i1) | (lane == i2), 1.0, 0.0).astype(F32)
    row = lax.broadcasted_iota(jnp.int32, (tm, tm), 0)
    col = lax.broadcasted_iota(jnp.int32, (tm, tm), 1)
    before = jnp.where(row > col, 1.0, 0.0).astype(BF16)
    excl = _dot(before, sel.astype(BF16)) + carry[0:1, :]
    r1 = jnp.where(lane == i1, excl, 0.0).sum(axis=-1, keepdims=True)
    r2 = jnp.where(lane == i2, excl, 0.0).sum(axis=-1, keepdims=True)
    carry[...] = carry[...] + sel.sum(axis=0, keepdims=True)
    info = jnp.where(lane == 0, i1, 0.0)
    info = jnp.where(lane == 1, i2, info)
    info = jnp.where(lane == 2, r1, info)
    info = jnp.where(lane == 3, r2, info)
    info = jnp.where(lane == 4, w1, info)
    info = jnp.where(lane == 5, w2, info)
    info_ref[...] = info
    cnt_ref[...] = carry[...]


def _route(logits):
    t = logits.shape[0]
    return pl.pallas_call(
        _route_kernel,
        grid=(t // ROUTE_TM,),
        in_specs=[pl.BlockSpec((ROUTE_TM, 128), lambda i: (i, 0))],
        out_specs=[pl.BlockSpec((ROUTE_TM, 128), lambda i: (i, 0)), pl.BlockSpec((8, 128), lambda i: (0, 0))],
        out_shape=[jax.ShapeDtypeStruct((t, 128), F32), jax.ShapeDtypeStruct((8, 128), F32)],
        scratch_shapes=[pltpu.VMEM((8, 128), F32)],
        compiler_params=_cp("arbitrary"),
        name="moe_route",
    )(logits)


SCATTER_TM = 256


def _scatter_kernel(d_ref, tok_ref, zero_ref, out_ref, sem):
    del zero_ref

    def copy(t, k):
        return pltpu.make_async_copy(tok_ref.at[pl.ds(t, 1)], out_ref.at[pl.ds(d_ref[0, 0, 2 * t + k], 1)], sem)

    def issue(t, carry):
        copy(t, 0).start()
        copy(t, 1).start()
        return carry

    def drain(t, carry):
        copy(t, 0).wait()
        copy(t, 1).wait()
        return carry

    lax.fori_loop(0, SCATTER_TM, issue, 0, unroll=8)
    lax.fori_loop(0, SCATTER_TM, drain, 0, unroll=8)


def _scatter_rows(dest, tok, n_pad):
    t, d = tok.shape
    n_t = t // SCATTER_TM
    return pl.pallas_call(
        _scatter_kernel,
        grid=(n_t,),
        in_specs=[pl.BlockSpec((1, 1, 2 * SCATTER_TM), lambda i: (i, 0, 0), memory_space=pltpu.SMEM),
                  pl.BlockSpec((SCATTER_TM, d), lambda i: (i, 0)),
                  pl.BlockSpec(memory_space=pl.ANY)],
        out_specs=pl.BlockSpec(memory_space=pl.ANY),
        out_shape=jax.ShapeDtypeStruct((n_pad, d), F32),
        scratch_shapes=[pltpu.SemaphoreType.DMA(())],
        input_output_aliases={2: 0},
        compiler_params=_cp("arbitrary"),
        name="moe_scatter",
    )(dest.reshape(n_t, 1, 2 * SCATTER_TM), tok, jnp.zeros((n_pad, d), F32))


EXP_CHUNK = D_FF_EXPERT // 4


def _expert_kernel(be_ref, nv_ref, x_ref, wg_ref, wu_ref, wd_ref, y_ref):
    del be_ref
    b = pl.program_id(0)
    j = pl.program_id(1)

    @pl.when(j == 0)
    def _():
        y_ref[...] = jnp.zeros_like(y_ref)

    @pl.when(b < nv_ref[0])
    def _():
        h = x_ref[...].astype(BF16)
        a = _silu(_dot(h, wg_ref[0])) * _dot(h, wu_ref[0])
        y_ref[...] += _dot(a.astype(BF16), wd_ref[0])


def _experts(x_pad, blk_expert, n_valid, w_up, w_down):
    n_pad, d = x_pad.shape
    n_blk = n_pad // MOE_BLOCK
    n_j = D_FF_EXPERT // EXP_CHUNK

    def jj(b, j, nv):
        return jnp.where(b < nv[0], j, n_j - 1)

    grid_spec = pltpu.PrefetchScalarGridSpec(
        num_scalar_prefetch=2,
        grid=(n_blk, n_j),
        in_specs=[pl.BlockSpec((MOE_BLOCK, d), lambda b, j, be, nv: (b, 0)),
                  pl.BlockSpec((1, d, EXP_CHUNK), lambda b, j, be, nv: (be[b], 0, jj(b, j, nv))),
                  pl.BlockSpec((1, d, EXP_CHUNK), lambda b, j, be, nv: (be[b], 0, n_j + jj(b, j, nv))),
                  pl.BlockSpec((1, EXP_CHUNK, d), lambda b, j, be, nv: (be[b], jj(b, j, nv), 0))],
        out_specs=pl.BlockSpec((MOE_BLOCK, d), lambda b, j, be, nv: (b, 0)),
    )
    return pl.pallas_call(
        _expert_kernel,
        grid_spec=grid_spec,
        out_shape=jax.ShapeDtypeStruct((n_pad, d), F32),
        compiler_params=_cp("arbitrary", "arbitrary"),
        name="moe_experts",
    )(blk_expert, n_valid, x_pad, w_up, w_up, w_down)


COMBINE_TM = 256


def _combine_kernel(d_ref, info_ref, x_ref, mod_ref, lng_ref, lnb_ref, y_ref, o_ref, gbuf, sem, *, alpha):
    def copy(t, k):
        return pltpu.make_async_copy(y_ref.at[pl.ds(d_ref[0, 0, 2 * t + k], 1)], gbuf.at[k, pl.ds(t, 1)], sem)

    def issue(t, carry):
        copy(t, 0).start()
        copy(t, 1).start()
        return carry

    def drain(t, carry):
        copy(t, 0).wait()
        copy(t, 1).wait()
        return carry

    lax.fori_loop(0, COMBINE_TM, issue, 0, unroll=8)
    lax.fori_loop(0, COMBINE_TM, drain, 0, unroll=8)
    m = mod_ref[0]
    info = info_ref[0]
    f = info[:, 4:5] * gbuf[0] + info[:, 5:6] * gbuf[1]
    o_ref[0] = _layer_norm(alpha * x_ref[0] + m[5:6] * f, lng_ref[...], lnb_ref[...])


def _combine(dest, info, x3, mod, lng, lnb, y_pad, alpha):
    g, r, d = x3.shape
    n_t = r // COMBINE_TM
    vec = pl.BlockSpec((1, d), lambda b, i: (0, 0))
    return pl.pallas_call(
        functools.partial(_combine_kernel, alpha=alpha),
        grid=(g, n_t),
        in_specs=[pl.BlockSpec((1, 1, 2 * COMBINE_TM), lambda b, i: (b * n_t + i, 0, 0), memory_space=pltpu.SMEM),
                  pl.BlockSpec((1, COMBINE_TM, 128), lambda b, i: (b, i, 0)),
                  pl.BlockSpec((1, COMBINE_TM, d), lambda b, i: (b, i, 0)),
                  pl.BlockSpec((1, 8, d), lambda b, i: (b, 0, 0)), vec, vec,
                  pl.BlockSpec(memory_space=pl.ANY)],
        out_specs=pl.BlockSpec((1, COMBINE_TM, d), lambda b, i: (b, i, 0)),
        out_shape=jax.ShapeDtypeStruct((g, r, d), F32),
        scratch_shapes=[pltpu.VMEM((2, COMBINE_TM, d), F32), pltpu.SemaphoreType.DMA(())],
        compiler_params=_cp("arbitrary", "arbitrary"),
        name="moe_combine_ln2",
    )(dest.reshape(g * n_t, 1, 2 * COMBINE_TM), info.reshape(g, r, 128), x3, mod, lng, lnb, y_pad)


def _moe_plan(info, counts_row):
    n_tok = info.shape[0]
    e = info[:, 0:2].astype(jnp.int32)
    rank = info[:, 2:4].astype(jnp.int32)
    counts = counts_row[:N_EXPERTS].astype(jnp.int32)
    padded = (counts + MOE_BLOCK - 1) // MOE_BLOCK * MOE_BLOCK
    pad_end = jnp.cumsum(padded)
    pad_start = pad_end - padded
    dest = pad_start[e] + rank
    n_blk = -(-n_tok * 2 // MOE_BLOCK) + N_EXPERTS
    blk_start = jnp.arange(n_blk, dtype=jnp.int32) * MOE_BLOCK
    blk_expert = jnp.minimum(jnp.sum(pad_end[None, :] <= blk_start[:, None], axis=1), N_EXPERTS - 1).astype(jnp.int32)
    n_valid = (pad_end[-1:] // MOE_BLOCK).astype(jnp.int32)
    return dest, blk_expert, n_valid, n_blk * MOE_BLOCK


def _pad_rows(a, rows):
    return jnp.concatenate([a, jnp.zeros((rows - a.shape[0],) + a.shape[1:], a.dtype)], axis=0)


def kernel(x, c, ctx, c_ctx, w_mod, b_mod, w_in, w_gate, b_gate, na_rpb, swa_sink, qk_gain_q, qk_gain_k, conv_w, conv_b, dt_bias, a_log, d_skip, ssm_norm_g, w_branch, w_out, ln1_g, ln1_b, ln2_g, ln2_b, ffn_w_up, ffn_w_down, moe_w_router, moe_b_router, moe_w_up, moe_w_down):
    bsz, seq, d = x.shape
    lc = ctx.shape[1]
    depth = w_mod.shape[0]
    alpha = (2 * depth) ** 0.25
    n_ctx = bsz * lc
    assert bsz < 8 and seq % 1024 == 0 and lc % 256 == 0 and d == D_MODEL

    s_rows = _pad_rows(jnp.concatenate([c, c_ctx[None, :]], axis=0), 8)
    mods = _modulation(s_rows, w_mod, b_mod).reshape(depth, 8, 6, d)
    mods = jnp.concatenate([mods, jnp.zeros((depth, 8, 2, d), F32)], axis=2)

    offs = [0, 512, 1024, 1536, 2048, 2176, 2304, 2816, 2944, 3072, 3584, 4608, 4624]
    seg = lambda k: w_in[:, :, offs[k]:offs[k + 1]]
    w_main = jnp.concatenate([seg(0), seg(1), seg(2), seg(3), seg(6), seg(9), seg(10), seg(4), seg(5), seg(7), seg(8)],
                             axis=-1).astype(BF16)
    zpad = jnp.zeros((depth, d, 128 - SSM_HEADS), F32)
    w_dt = jnp.concatenate([seg(11)[:, :, :SSM_HEADS], zpad, seg(11)[:, :, SSM_HEADS:], zpad], axis=-1).astype(BF16)

    w_gate_b = w_gate.astype(BF16)
    w_branch_b = w_branch.astype(BF16)
    w_out_b = w_out.astype(BF16)
    ffn_up_b = ffn_w_up.astype(BF16)
    ffn_down_b = ffn_w_down.astype(BF16)
    moe_up_b = moe_w_up.astype(BF16)
    moe_down_b = moe_w_down.astype(BF16)

    rope_tabs = _rope_tables(seq)
    a_neg = -jnp.exp(a_log.astype(F32))
    pad8 = lambda v: jnp.concatenate([v, jnp.zeros(v.shape[:-1] + (128 - SSM_HEADS,), F32)], axis=-1)

    ctx2 = ctx.reshape(1, n_ctx, d)
    for l in range(depth):
        with_ctx = l < depth - 1
        mod_lat = mods[l, :bsz]
        mod_ctx = mods[l, bsz:bsz + 1]

        p, dt_raw = _inproj(x, mod_lat, w_main[l], w_dt[l], 1024)
        pc, dtc_raw = _inproj(ctx2, mod_ctx, w_main[l], w_dt[l], n_ctx)
        pc = pc.reshape(bsz, lc, P_COLS)
        dtc_raw = dtc_raw.reshape(bsz, lc, DT_COLS)

        gq = jnp.tile(qk_gain_q[l].astype(F32), 2)[None, :]
        gk = jnp.tile(qk_gain_k[l].astype(F32), 2)[None, :]
        swq, gaq_pad, swk, gak, gavt = _prep(p, rope_tabs, gq, gk, 1024, True, True, GA_TK)
        swq_c, gaq_c, swk_c, gak_c, gavt_c = _prep(pc, rope_tabs, gq, gk, lc, False, False, lc)

        sink_tab = jnp.broadcast_to(swa_sink[l].astype(F32)[:, None], (8, 128))
        o_a = _na_attention(p, pc, _na_bias_table(na_rpb[l]))
        o_b = _swa_attention(swq, swk, p, swk_c, pc, sink_tab)
        o_c = _ga_attention(gaq_pad, gak, gavt, gak_c, gavt_c)

        w8 = _pad_rows(conv_w[l].astype(F32), 8)
        cb = conv_b[l].astype(F32)[None, :]
        u = _conv(p, w8, cb, 512)
        uc = _conv(pc, w8, cb, lc)
        dtb = jnp.concatenate([pad8(dt_bias[l, 0].astype(F32)), pad8(dt_bias[l, 1].astype(F32))])[None, :]
        a_full = jnp.concatenate([pad8(a_neg[l, 0]), pad8(a_neg[l, 1])])[None, :]
        h_zero = jnp.zeros((bsz, 2, 2, SSM_STATE, 256), F32)
        yc, h_ctx = _ssd(uc, dtc_raw, dtb, a_full, h_zero, lc)
        y, _ = _ssd(u, dt_raw, dtb, a_full, h_ctx, 256)
        dsk = jnp.repeat(d_skip[l].astype(F32), HEAD_DIM)[None, :]
        ng = ssm_norm_g[l].astype(F32)[None, :]
        o_d = _ssd_finish(y, u, p, dsk, ng, 1024)

        is_moe = l % 2 == 1
        i = l // 2
        router = None
        if is_moe:
            wr = jnp.concatenate([moe_w_router[i].astype(F32), jnp.zeros((d, 128 - N_EXPERTS), F32)], axis=-1)
            br = jnp.concatenate([moe_b_router[i].astype(F32), jnp.full((128 - N_EXPERTS,), NEG_INF, F32)])[None, :]
            wr_hi = wr.astype(BF16)
            wr_lo = (wr - wr_hi.astype(F32)).astype(BF16)
            router = (jnp.stack([wr_hi, wr_lo]), br)
        lng1, lnb1 = ln1_g[l].astype(F32)[None, :], ln1_b[l].astype(F32)[None, :]
        lng2, lnb2 = ln2_g[l].astype(F32)[None, :], ln2_b[l].astype(F32)[None, :]
        bg = b_gate[l].astype(F32)[:, None, :]
        res = _merge(x, mod_lat, (o_a, o_b, o_c, o_d), w_gate_b[l], bg, w_branch_b[l], w_out_b[l], lng1, lnb1,
                     alpha, 512, router)
        if with_ctx:
            o_ac = _ctx_attention(pc, C512_NAQ, pc, C512_NAK, pc, C512_NAV, 8, sink_tab, True, False)
            o_bc = _ctx_attention(swq_c, 0, swk_c, 0, pc, C128_SWV, 2, sink_tab, False, True)
            o_cc = _ctx_attention(gaq_c, 0, gak_c, 0, pc, C128_GAV, 2, sink_tab, False, False)
            o_dc = _ssd_finish(yc, uc, pc, dsk, ng, lc)
            to2 = lambda t: t.reshape(1, n_ctx, BRANCH_W)
            res_c = _merge(ctx2, mod_ctx, (to2(o_ac), to2(o_bc), to2(o_cc), to2(o_dc)), w_gate_b[l], bg,
                           w_branch_b[l], w_out_b[l], lng1, lnb1, alpha, min(512, n_ctx), router)

        if not is_moe:
            x = _ffn(res[0], mod_lat, ffn_up_b[i], ffn_down_b[i], lng2, lnb2, alpha, 1024)
            if with_ctx:
                ctx2 = _ffn(res_c[0], mod_ctx, ffn_up_b[i], ffn_down_b[i], lng2, lnb2, alpha, n_ctx)
        else:
            x1, h2, lg = res
            tok = h2.reshape(bsz * seq, d)
            lgs = lg.reshape(bsz * seq, 128)
            if with_ctx:
                x1c, h2c, lgc = res_c
                tok = jnp.concatenate([tok, h2c.reshape(n_ctx, d)], axis=0)
                lgs = jnp.concatenate([lgs, lgc.reshape(n_ctx, 128)], axis=0)
            info, cnt = _route(lgs)
            dest, blk_expert, n_valid, n_pad = _moe_plan(info, cnt[0])
            x_pad = _scatter_rows(dest, tok, n_pad)
            y_pad = _experts(x_pad, blk_expert, n_valid, moe_up_b[i], moe_down_b[i])
            n_lat = bsz * seq
            x = _combine(dest[:n_lat], info[:n_lat], x1, mod_lat, lng2, lnb2, y_pad, alpha)
            if with_ctx:
                ctx2 = _combine(dest[n_lat:], info[n_lat:], x1c, mod_ctx, lng2, lnb2, y_pad, alpha)
    return x
```

```python
import functools
import math

import jax
import jax.numpy as jnp
from jax import lax
from jax.experimental import pallas as pl
from jax.experimental.pallas import tpu as pltpu

F32 = jnp.float32
BF16 = jnp.bfloat16
HI = lax.Precision.HIGHEST

D_MODEL = 1024
GRID_W = 64
HEAD_DIM = 64
ATTN_SCALE = HEAD_DIM ** -0.5
ROPE_THETA = 10000.0
NEG_INF = -1e30
NA_HEADS = 8
NA_KH = 8
NA_KW = 16
SWA_WINDOW = 128
SSM_HEADS = 8
SSM_INNER = 512
SSM_STATE = 128
SSM_CONV = 5
N_BRANCH = 4
BRANCH_W = 512
D_FF = 2816
N_EXPERTS = 8
D_FF_EXPERT = 3584
MOE_BLOCK = 512
LN_EPS = 1e-5
RMS_EPS = 1e-6

P_COLS = 4608
C512_NAQ, C512_NAK, C512_NAV, C512_SWQ, C512_GAQ, C512_Z = 0, 1, 2, 3, 4, 5
C1024_XBC = 3
C128_SWK, C128_SWV, C128_GAK, C128_GAV = 32, 33, 34, 35
DT_COLS = 256

VMEM_LIMIT = 48 * 1024 * 1024


def _cp(*sem):
    return pltpu.CompilerParams(dimension_semantics=sem, vmem_limit_bytes=VMEM_LIMIT)


def _sigmoid(x):
    return 1.0 / (1.0 + jnp.exp(-x))


def _silu(x):
    return x * _sigmoid(x)


def _softplus(x):
    return jnp.maximum(x, 0.0) + jnp.log1p(jnp.exp(-jnp.abs(x)))


def _layer_norm(y, g, b):
    mu = jnp.mean(y, axis=-1, keepdims=True)
    yc = y - mu
    var = jnp.mean(yc * yc, axis=-1, keepdims=True)
    return yc * lax.rsqrt(var + LN_EPS) * g + b


def _dot(a, b):
    return jnp.dot(a, b, preferred_element_type=F32)


def _dot_nt(a, b):
    return lax.dot_general(a, b, (((1,), (1,)), ((), ())), preferred_element_type=F32)


def _mod_kernel(s_ref, w_ref, b_ref, o_ref):
    s = _silu(s_ref[...])
    o_ref[0] = jnp.dot(s, w_ref[0], precision=HI, preferred_element_type=F32) + b_ref[0]


def _modulation(s_rows, w_mod, b_mod):
    n_l, d, d6 = w_mod.shape
    tn = 1024
    return pl.pallas_call(
        _mod_kernel,
        grid=(n_l, d6 // tn),
        in_specs=[pl.BlockSpec((8, d), lambda l, j: (0, 0)),
                  pl.BlockSpec((1, d, tn), lambda l, j: (l, 0, j)),
                  pl.BlockSpec((1, 1, tn), lambda l, j: (l, 0, j))],
        out_specs=pl.BlockSpec((1, 8, tn), lambda l, j: (l, 0, j)),
        out_shape=jax.ShapeDtypeStruct((n_l, 8, d6), F32),
        compiler_params=_cp("parallel", "parallel"),
        name="modulation",
    )(s_rows, w_mod, b_mod.reshape(n_l, 1, d6))


def _inproj_kernel(x_ref, mod_ref, w_ref, wdt_ref, p_ref, dt_ref, h_scr):
    @pl.when(pl.program_id(2) == 0)
    def _():
        m = mod_ref[0]
        h = (x_ref[0] * (1.0 + m[1:2]) + m[0:1]).astype(BF16)
        h_scr[...] = h
        dt_ref[0] = _dot(h, wdt_ref[...])

    p_ref[0] = _dot(h_scr[...], w_ref[...]).astype(BF16)


def _inproj(x3, mod, w_main, w_dt, tm):
    g, r, d = x3.shape
    tn = 1152
    return pl.pallas_call(
        _inproj_kernel,
        grid=(g, r // tm, P_COLS // tn),
        in_specs=[pl.BlockSpec((1, tm, d), lambda b, i, j: (b, i, 0)),
                  pl.BlockSpec((1, 8, d), lambda b, i, j: (b, 0, 0)),
                  pl.BlockSpec((d, tn), lambda b, i, j: (0, j)),
                  pl.BlockSpec((d, DT_COLS), lambda b, i, j: (0, 0))],
        out_specs=[pl.BlockSpec((1, tm, tn), lambda b, i, j: (b, i, j)),
                   pl.BlockSpec((1, tm, DT_COLS), lambda b, i, j: (b, i, 0))],
        out_shape=[jax.ShapeDtypeStruct((g, r, P_COLS), BF16),
                   jax.ShapeDtypeStruct((g, r, DT_COLS), F32)],
        scratch_shapes=[pltpu.VMEM((tm, d), BF16)],
        compiler_params=_cp("parallel", "parallel", "arbitrary"),
        name="inproj",
    )(x3, mod, w_main, w_dt)


GA_TK = 512
LOG2E = math.log2(math.e)


def _prep_kernel(swq_ref, gaq_ref, swk_ref, gak_ref, gav_ref, cos_ref, sa_ref, sb_ref, gq_ref, gk_ref,
                 oswq, ogaq, oswk, ogak, ogavt, *, rope, pad_q, vt_chunk):
    row = lax.broadcasted_iota(jnp.int32, (128, 128), 0) // HEAD_DIM
    col = lax.broadcasted_iota(jnp.int32, (128, 128), 1) // HEAD_DIM
    head_mean = jnp.where(row == col, 1.0 / HEAD_DIM, 0.0).astype(F32)
    first_half = lax.broadcasted_iota(jnp.int32, (1, 128), 1) < HEAD_DIM

    def rot(x):
        if not rope:
            return x
        return (x * cos_ref[...] + pltpu.roll(x, 128 - HEAD_DIM // 2, 1) * sa_ref[...]
                + pltpu.roll(x, HEAD_DIM // 2, 1) * sb_ref[...])

    def rms(x, g):
        ms = jnp.dot(x * x, head_mean, precision=HI, preferred_element_type=F32)
        return x * lax.rsqrt(ms + RMS_EPS) * g

    def put_padded(o_ref, x, c):
        swapped = pltpu.roll(x, HEAD_DIM, 1)
        keep = first_half if c < 2 else jnp.logical_not(first_half)
        even, odd = (x, swapped) if c < 2 else (swapped, x)
        o_ref[0, :, 2 * c * 128:(2 * c + 1) * 128] = jnp.where(keep, even, 0.0).astype(BF16)
        o_ref[0, :, (2 * c + 1) * 128:(2 * c + 2) * 128] = jnp.where(keep, odd, 0.0).astype(BF16)

    for c in range(4):
        sl = slice(c * 128, (c + 1) * 128)
        sq_c = rot(swq_ref[0, :, sl].astype(F32)) * ATTN_SCALE
        gq_c = rot(rms(gaq_ref[0, :, sl].astype(F32), gq_ref[...]))
        if pad_q:
            put_padded(oswq, sq_c, c)
            put_padded(ogaq, gq_c * (ATTN_SCALE * LOG2E), c)
        else:
            oswq[0, :, sl] = sq_c.astype(BF16)
            ogaq[0, :, sl] = (gq_c * ATTN_SCALE).astype(BF16)
    oswk[0] = rot(swk_ref[0].astype(F32)).astype(BF16)
    ogak[0] = rot(rms(gak_ref[0].astype(F32), gk_ref[...])).astype(BF16)
    for cc in range(gav_ref.shape[1] // vt_chunk):
        ogavt[0, cc] = gav_ref[0, cc * vt_chunk:(cc + 1) * vt_chunk, :].astype(F32).T.astype(BF16)


def _prep(p, tables, gq, gk, tm, rope, pad_q, vt_chunk):
    g, r, _ = p.shape
    cosf, sin_a, sin_b = tables
    tab = pl.BlockSpec((tm, 128), lambda b, i: (i, 0))
    vec = pl.BlockSpec((1, 128), lambda b, i: (0, 0))
    qw = 1024 if pad_q else 512
    n_vt = tm // vt_chunk
    return pl.pallas_call(
        functools.partial(_prep_kernel, rope=rope, pad_q=pad_q, vt_chunk=vt_chunk),
        grid=(g, r // tm),
        in_specs=[pl.BlockSpec((1, tm, 512), lambda b, i: (b, i, C512_SWQ)),
                  pl.BlockSpec((1, tm, 512), lambda b, i: (b, i, C512_GAQ)),
                  pl.BlockSpec((1, tm, 128), lambda b, i: (b, i, C128_SWK)),
                  pl.BlockSpec((1, tm, 128), lambda b, i: (b, i, C128_GAK)),
                  pl.BlockSpec((1, tm, 128), lambda b, i: (b, i, C128_GAV)),
                  tab, tab, tab, vec, vec],
        out_specs=[pl.BlockSpec((1, tm, qw), lambda b, i: (b, i, 0)),
                   pl.BlockSpec((1, tm, qw), lambda b, i: (b, i, 0)),
                   pl.BlockSpec((1, tm, 128), lambda b, i: (b, i, 0)),
                   pl.BlockSpec((1, tm, 128), lambda b, i: (b, i, 0)),
                   pl.BlockSpec((1, n_vt, 128, vt_chunk), lambda b, i: (b, i, 0, 0))],
        out_shape=[jax.ShapeDtypeStruct((g, r, qw), BF16), jax.ShapeDtypeStruct((g, r, qw), BF16),
                   jax.ShapeDtypeStruct((g, r, 128), BF16), jax.ShapeDtypeStruct((g, r, 128), BF16),
                   jax.ShapeDtypeStruct((g, r // vt_chunk, 128, vt_chunk), BF16)],
        compiler_params=_cp("parallel", "parallel"),
        name="qk_prep",
    )(p, p, p, p, p, cosf, sin_a, sin_b, gq, gk)


def _rope_tables(n_tok):
    t = jnp.arange(n_tok, dtype=jnp.int32)
    row = (t // GRID_W).astype(F32)
    col = (t % GRID_W).astype(F32)
    n_freq = HEAD_DIM // 4
    inv_freq = ROPE_THETA ** (-jnp.arange(n_freq, dtype=F32) / n_freq)
    ang = jnp.concatenate([row[:, None] * inv_freq, col[:, None] * inv_freq], axis=-1)
    cos, sin = jnp.cos(ang), jnp.sin(ang)
    zero = jnp.zeros_like(sin)
    cosf = jnp.tile(jnp.concatenate([cos, cos], -1), (1, 2))
    sin_a = jnp.tile(jnp.concatenate([-sin, zero], -1), (1, 2))
    sin_b = jnp.tile(jnp.concatenate([zero, sin], -1), (1, 2))
    return cosf, sin_a, sin_b


def _conv_kernel(prev_ref, cur_ref, next_ref, w_ref, b_ref, o_ref, *, tc):
    i = pl.program_id(1)
    n = pl.num_programs(1)
    prev = jnp.where(i > 0, prev_ref[0].astype(F32), 0.0)
    nxt = jnp.where(i < n - 1, next_ref[0].astype(F32), 0.0)
    ext = jnp.concatenate([prev, cur_ref[0].astype(F32), nxt], axis=0)
    rows = tc + 32
    acc = b_ref[...] + jnp.zeros((tc, ext.shape[1]), F32)
    for j in range(SSM_CONV):
        shift = (SSM_CONV // 2 - j) % rows
        sh = ext if shift == 0 else pltpu.roll(ext, shift, 0)
        acc = acc + sh[16:16 + tc] * w_ref[j:j + 1, :]
    o_ref[0] = _silu(acc).astype(BF16)


def _conv(p, w8, b, tc):
    g, r, _ = p.shape
    nb16 = r // 16
    k = tc // 16
    return pl.pallas_call(
        functools.partial(_conv_kernel, tc=tc),
        grid=(g, r // tc),
        in_specs=[pl.BlockSpec((1, 16, 1024), lambda b_, i: (b_, jnp.maximum(i * k - 1, 0), C1024_XBC)),
                  pl.BlockSpec((1, tc, 1024), lambda b_, i: (b_, i, C1024_XBC)),
                  pl.BlockSpec((1, 16, 1024), lambda b_, i: (b_, jnp.minimum((i + 1) * k, nb16 - 1), C1024_XBC)),
                  pl.BlockSpec((8, 1024), lambda b_, i: (0, 0)),
                  pl.BlockSpec((1, 1024), lambda b_, i: (0, 0))],
        out_specs=pl.BlockSpec((1, tc, 1024), lambda b_, i: (b_, i, 0)),
        out_shape=jax.ShapeDtypeStruct((g, r, 1024), BF16),
        compiler_params=_cp("parallel", "parallel"),
        name="dwconv_silu",
    )(p, p, p, w8, b)


def _softmax_out(units):
    probs = []
    for parts, sink in units:
        m = parts[0][0].max(axis=-1, keepdims=True)
        for s, _ in parts[1:]:
            m = jnp.maximum(m, s.max(axis=-1, keepdims=True))
        if sink is not None:
            m = jnp.maximum(m, sink)
        l = jnp.exp(sink - m) if sink is not None else 0.0
        es = []
        for s, _ in parts:
            e = jnp.exp(s - m)
            l = l + e.sum(axis=-1, keepdims=True)
            es.append(e.astype(BF16))
        probs.append((es, l))
    outs = []
    for (parts, _), (es, l) in zip(units, probs):
        acc = _dot(es[0], parts[0][1])
        for e, (_, v) in zip(es[1:], parts[1:]):
            acc = acc + _dot(e, v)
        outs.append(acc / l)
    return outs


NA_ROWS_PER_STEP = 8


def _na_kernel(q_ref, k_ref, v_ref, kc_ref, vc_ref, bias_ref, o_ref, *, n_rows):
    blk = pl.program_id(2)
    kc = kc_ref[0]
    vc = vc_ref[0]
    first = lax.broadcasted_iota(jnp.int32, (1, 128), 1) < HEAD_DIM
    keeps = (first, jnp.logical_not(first))
    vcs = [jnp.where(keep, vc, 0) for keep in keeps]

    units = []
    for rr in range(NA_ROWS_PER_STEP):
        r = blk * NA_ROWS_PER_STEP + rr
        start = jnp.clip(r - NA_KH // 2, 0, n_rows - NA_KH)
        off = start - r + NA_KH - 1
        q = q_ref[0, rr * GRID_W:(rr + 1) * GRID_W, :] * ATTN_SCALE
        k0 = pl.multiple_of(start * GRID_W, GRID_W)
        kw = k_ref[0, pl.ds(k0, NA_KH * GRID_W), :]
        vw = v_ref[0, pl.ds(k0, NA_KH * GRID_W), :]
        for hh in range(2):
            qh = jnp.where(keeps[hh], q, 0)
            s_loc = _dot_nt(qh, kw) + bias_ref[off, hh]
            s_ctx = _dot_nt(qh, kc)
            units.append(([(s_loc, jnp.where(keeps[hh], vw, 0)), (s_ctx, vcs[hh])], None))
    outs = _softmax_out(units)
    for rr in range(NA_ROWS_PER_STEP):
        o_ref[0, rr * GRID_W:(rr + 1) * GRID_W, :] = (outs[2 * rr] + outs[2 * rr + 1]).astype(BF16)


def _na_bias_table(rpb):
    o = jnp.arange(NA_KH)
    kr = jnp.arange(NA_KH)
    dr = o[:, None] + kr[None, :]
    c = jnp.arange(GRID_W)
    kcol = jnp.arange(GRID_W)
    cs = jnp.clip(c - NA_KW // 2, 0, GRID_W - NA_KW)
    dc = kcol[None, :] - c[:, None] + NA_KW - 1
    valid = (kcol[None, :] >= cs[:, None]) & (kcol[None, :] < cs[:, None] + NA_KW)
    tbl = rpb[:, dr][:, :, :, jnp.clip(dc, 0, 2 * NA_KW - 2)]
    tbl = jnp.where(valid[None, None, None], tbl.astype(F32), NEG_INF)
    return tbl.transpose(1, 0, 3, 2, 4).reshape(NA_KH, NA_HEADS, GRID_W, NA_KH * GRID_W)


def _na_attention(p, pc, bias):
    b, s, _ = p.shape
    lc = pc.shape[1]
    n_rows = s // GRID_W
    tq = NA_ROWS_PER_STEP * GRID_W
    return pl.pallas_call(
        functools.partial(_na_kernel, n_rows=n_rows),
        grid=(b, NA_HEADS // 2, s // tq),
        in_specs=[pl.BlockSpec((1, tq, 128), lambda b_, h, i: (b_, i, 4 * C512_NAQ + h)),
                  pl.BlockSpec((1, s, 128), lambda b_, h, i: (b_, 0, 4 * C512_NAK + h)),
                  pl.BlockSpec((1, s, 128), lambda b_, h, i: (b_, 0, 4 * C512_NAV + h)),
                  pl.BlockSpec((1, lc, 128), lambda b_, h, i: (b_, 0, 4 * C512_NAK + h)),
                  pl.BlockSpec((1, lc, 128), lambda b_, h, i: (b_, 0, 4 * C512_NAV + h)),
                  pl.BlockSpec((NA_KH, 2, GRID_W, NA_KH * GRID_W), lambda b_, h, i: (0, h, 0, 0))],
        out_specs=pl.BlockSpec((1, tq, 128), lambda b_, h, i: (b_, i, h)),
        out_shape=jax.ShapeDtypeStruct((b, s, 512), BF16),
        compiler_params=_cp("parallel", "parallel", "arbitrary"),
        name="na_attention",
    )(p, p, p, pc, pc, bias)


SWA_TQ = 256
SWA_BAND = SWA_TQ + 2 * SWA_WINDOW


def _swa_kernel(q_ref, k_ref, v_ref, kc_ref, vc_ref, sink_ref, o_ref, *, n_tok):
    n = pl.program_id(1)
    start = pl.multiple_of(jnp.clip(n * SWA_TQ - SWA_WINDOW, 0, n_tok - SWA_BAND), SWA_WINDOW)
    kb = k_ref[0, pl.ds(start, SWA_BAND), :]
    vb = v_ref[0, pl.ds(start, SWA_BAND), :]
    kc = kc_ref[0]
    vc = vc_ref[0]
    kpos = start + lax.broadcasted_iota(jnp.int32, (SWA_BAND, SWA_TQ), 0)
    qpos = n * SWA_TQ + lax.broadcasted_iota(jnp.int32, (SWA_BAND, SWA_TQ), 1)
    in_win = jnp.abs(kpos - qpos) <= SWA_WINDOW
    vbt = vb.astype(F32).T.astype(BF16)
    vct = vc.astype(F32).T.astype(BF16)
    scores = []
    for h in range(8):
        qh = q_ref[0, :, h * 128:(h + 1) * 128]
        scores.append((jnp.where(in_win, _dot_nt(kb, qh), NEG_INF), _dot_nt(kc, qh)))
    probs = []
    for h, (s_loc, s_ctx) in enumerate(scores):
        sink = sink_ref[h:h + 1, 0:1]
        m = jnp.maximum(jnp.maximum(s_loc.max(axis=0, keepdims=True), s_ctx.max(axis=0, keepdims=True)), sink)
        e_loc = jnp.exp(s_loc - m)
        e_ctx = jnp.exp(s_ctx - m)
        l = jnp.exp(sink - m) + e_loc.sum(axis=0, keepdims=True) + e_ctx.sum(axis=0, keepdims=True)
        probs.append((e_loc.astype(BF16), e_ctx.astype(BF16), l))
    outs = []
    for h, (e_loc, e_ctx, l) in enumerate(probs):
        rows = slice((h // 4) * HEAD_DIM, (h // 4 + 1) * HEAD_DIM)
        outs.append((_dot(vbt[rows, :], e_loc) + _dot(vct[rows, :], e_ctx)) / l)
    o_ref[0] = jnp.concatenate(outs, axis=0).T.astype(BF16)


def _swa_attention(q, k, p, kc, pc, sink_tab):
    b, s, _ = k.shape
    lc = kc.shape[1]
    return pl.pallas_call(
        functools.partial(_swa_kernel, n_tok=s),
        grid=(b, s // SWA_TQ),
        in_specs=[pl.BlockSpec((1, SWA_TQ, 1024), lambda b_, i: (b_, i, 0)),
                  pl.BlockSpec((1, s, 128), lambda b_, i: (b_, 0, 0)),
                  pl.BlockSpec((1, s, 128), lambda b_, i: (b_, 0, C128_SWV)),
                  pl.BlockSpec((1, lc, 128), lambda b_, i: (b_, 0, 0)),
                  pl.BlockSpec((1, lc, 128), lambda b_, i: (b_, 0, C128_SWV)),
                  pl.BlockSpec((8, 128), lambda b_, i: (0, 0))],
        out_specs=pl.BlockSpec((1, SWA_TQ, 512), lambda b_, i: (b_, i, 0)),
        out_shape=jax.ShapeDtypeStruct((b, s, 512), BF16),
        compiler_params=_cp("parallel", "arbitrary"),
        name="swa_attention",
    )(q, k, p, kc, pc, sink_tab)


GA_TQ = 256


def _ga_kernel(q_ref, k_ref, vt_ref, kc_ref, vct_ref, o_ref, acc_scr, *, n_chunks):
    acc_scr[...] = jnp.zeros_like(acc_scr)

    def step(carry, kk, vt):
        m_all, l_all = carry

        def scores(h):
            return _dot_nt(kk, q_ref[0, :, h * 128:(h + 1) * 128])

        m_rows, l_rows = [], []
        sts = [scores(h) for h in range(8)]
        for h in range(8):
            st = sts[h]
            g = h // 4
            m_old = m_all[h:h + 1, :]
            m_new = jnp.maximum(m_old, st.max(axis=0, keepdims=True))
            a = jnp.exp2(m_old - m_new)
            e = jnp.exp2(st - m_new)
            l_rows.append(a * l_all[h:h + 1, :] + e.sum(axis=0, keepdims=True))
            m_rows.append(m_new)
            rows = slice(h * HEAD_DIM, (h + 1) * HEAD_DIM)
            acc_scr[rows, :] = a * acc_scr[rows, :] + _dot(vt[g * HEAD_DIM:(g + 1) * HEAD_DIM, :], e.astype(BF16))
        return jnp.concatenate(m_rows, axis=0), jnp.concatenate(l_rows, axis=0)

    def chunk(c, carry):
        k0 = pl.multiple_of(c * GA_TK, GA_TK)
        return step(carry, k_ref[0, pl.ds(k0, GA_TK), :], vt_ref[0, c])

    init = (jnp.full((8, GA_TQ), NEG_INF, F32), jnp.zeros((8, GA_TQ), F32))
    carry = lax.fori_loop(0, n_chunks, chunk, init)
    _, l_fin = step(carry, kc_ref[0], vct_ref[0, 0])
    inv = 1.0 / l_fin
    out_t = jnp.concatenate([acc_scr[h * HEAD_DIM:(h + 1) * HEAD_DIM, :] * inv[h:h + 1, :] for h in range(8)], axis=0)
    o_ref[0] = out_t.T.astype(BF16)


def _ga_attention(q_pad, k, vt, kc, vct):
    b, s, _ = k.shape
    lc = kc.shape[1]
    n_chunks = s // GA_TK
    return pl.pallas_call(
        functools.partial(_ga_kernel, n_chunks=n_chunks),
        grid=(b, s // GA_TQ),
        in_specs=[pl.BlockSpec((1, GA_TQ, 1024), lambda b_, i: (b_, i, 0)),
                  pl.BlockSpec((1, s, 128), lambda b_, i: (b_, 0, 0)),
                  pl.BlockSpec((1, n_chunks, 128, GA_TK), lambda b_, i: (b_, 0, 0, 0)),
                  pl.BlockSpec((1, lc, 128), lambda b_, i: (b_, 0, 0)),
                  pl.BlockSpec((1, 1, 128, lc), lambda b_, i: (b_, 0, 0, 0))],
        out_specs=pl.BlockSpec((1, GA_TQ, 512), lambda b_, i: (b_, i, 0)),
        out_shape=jax.ShapeDtypeStruct((b, s, 512), BF16),
        scratch_shapes=[pltpu.VMEM((8 * HEAD_DIM, GA_TQ), F32)],
        compiler_params=_cp("parallel", "arbitrary"),
        name="ga_attention",
    )(q_pad, k, vt, kc, vct)


def _ctx_attn_kernel(q_ref, k_ref, v_ref, sink_ref, o_ref, *, n_kv, q_scale, use_sink):
    rep = 8 // n_kv
    units = []
    for h in range(8):
        g = h // rep
        gs = slice(g * HEAD_DIM, (g + 1) * HEAD_DIM)
        qh = q_ref[0, :, h * HEAD_DIM:(h + 1) * HEAD_DIM]
        if q_scale:
            qh = qh * ATTN_SCALE
        s = _dot_nt(qh, k_ref[0, :, gs])
        units.append(([(s, v_ref[0, :, gs])], sink_ref[h:h + 1, 0:1] if use_sink else None))
    o_ref[0] = jnp.concatenate(_softmax_out(units), axis=-1).astype(BF16)


def _ctx_attention(q, qcol, k, kcol, v, vcol, n_kv, sink_tab, q_scale, use_sink):
    b, lc, _ = q.shape
    kvw = n_kv * HEAD_DIM
    return pl.pallas_call(
        functools.partial(_ctx_attn_kernel, n_kv=n_kv, q_scale=q_scale, use_sink=use_sink),
        grid=(b,),
        in_specs=[pl.BlockSpec((1, lc, 512), lambda b_: (b_, 0, qcol)),
                  pl.BlockSpec((1, lc, kvw), lambda b_: (b_, 0, kcol)),
                  pl.BlockSpec((1, lc, kvw), lambda b_: (b_, 0, vcol)),
                  pl.BlockSpec((8, 128), lambda b_: (0, 0))],
        out_specs=pl.BlockSpec((1, lc, 512), lambda b_: (b_, 0, 0)),
        out_shape=jax.ShapeDtypeStruct((b, lc, 512), BF16),
        compiler_params=_cp("parallel"),
        name="ctx_attention",
    )(q, k, v, sink_tab)


def _ssd_kernel(u_ref, dt_ref, dtb_ref, a_ref, h0_ref, y_ref, hT_ref, st, *, q_len):
    d = pl.program_id(1)
    c = pl.program_id(2)

    @pl.when(c == 0)
    def _():
        st[...] = h0_ref[0, 0]

    u = u_ref[0]
    dt = _softplus(dt_ref[0] + dtb_ref[...])
    da = dt * a_ref[...]
    row = lax.broadcasted_iota(jnp.int32, (q_len, q_len), 0)
    col = lax.broadcasted_iota(jnp.int32, (q_len, q_len), 1)
    causal = (row - col) * (1 - 2 * d) >= 0
    acum = jnp.dot(causal.astype(F32), da, precision=HI, preferred_element_type=F32)
    acum_t = acum.T
    total = jnp.sum(da, axis=0, keepdims=True)
    cbs, bg_ts, s_gs, y_offs = [], [], [], []
    for g in range(2):
        bg = u[:, SSM_INNER + g * SSM_STATE:SSM_INNER + (g + 1) * SSM_STATE]
        cg = u[:, SSM_INNER + (2 + g) * SSM_STATE:SSM_INNER + (3 + g) * SSM_STATE]
        cbs.append(_dot_nt(cg, bg))
        bg_ts.append(bg.astype(F32).T.astype(BF16))
        s_gs.append(st[g])
        y_offs.append(_dot(cg, s_gs[g].astype(BF16)))
    first = lax.broadcasted_iota(jnp.int32, (1, 128), 1) < HEAD_DIM
    mixes, xms, xdecs, e_acs, e_tots = [], [], [], [], []
    for pr in range(SSM_HEADS // 2):
        h0, h1 = 2 * pr, 2 * pr + 1
        ac0, ac1 = acum[:, h0:h0 + 1], acum[:, h1:h1 + 1]
        ac_pair = jnp.where(first, ac0, ac1)
        tot_pair = jnp.where(first, total[:, h0:h0 + 1], total[:, h1:h1 + 1])
        xdt = u[:, pr * 128:(pr + 1) * 128].astype(F32) * jnp.where(first, dt[:, h0:h0 + 1], dt[:, h1:h1 + 1])
        for h, ac, keep in ((h0, ac0, first), (h1, ac1, jnp.logical_not(first))):
            seg = jnp.exp(jnp.where(causal, ac - acum_t[h:h + 1, :], NEG_INF))
            mixes.append((cbs[pr // 2] * seg).astype(BF16))
            xms.append(jnp.where(keep, xdt, 0.0).astype(BF16))
        xdecs.append((xdt * jnp.exp(tot_pair - ac_pair)).astype(BF16))
        e_acs.append(jnp.exp(ac_pair))
        e_tots.append(jnp.exp(tot_pair))
    y_diag = [_dot(mixes[2 * pr], xms[2 * pr]) + _dot(mixes[2 * pr + 1], xms[2 * pr + 1])
              for pr in range(SSM_HEADS // 2)]
    upd = [_dot(bg_ts[pr // 2], xdecs[pr]) for pr in range(SSM_HEADS // 2)]
    ys = []
    for g in range(2):
        new_cols = []
        for pp in range(2):
            pr = 2 * g + pp
            ls = slice(pp * 128, (pp + 1) * 128)
            ys.append(y_diag[pr] + y_offs[g][:, ls] * e_acs[pr])
            new_cols.append(s_gs[g][:, ls] * e_tots[pr] + upd[pr])
        st[g] = jnp.concatenate(new_cols, axis=-1)
    y_ref[0, 0] = jnp.concatenate(ys, axis=-1)

    @pl.when(c == pl.num_programs(2) - 1)
    def _():
        hT_ref[0, 0] = st[...]


def _ssd(u, dt_raw, dtb, a_full, h0, q_len):
    b, l, _ = u.shape
    n_c = l // q_len

    def cidx(d, c):
        return c + d * (n_c - 1 - 2 * c)

    return pl.pallas_call(
        functools.partial(_ssd_kernel, q_len=q_len),
        grid=(b, 2, n_c),
        in_specs=[pl.BlockSpec((1, q_len, 1024), lambda b_, d, c: (b_, cidx(d, c), 0)),
                  pl.BlockSpec((1, q_len, 128), lambda b_, d, c: (b_, cidx(d, c), d)),
                  pl.BlockSpec((1, 128), lambda b_, d, c: (0, d)),
                  pl.BlockSpec((1, 128), lambda b_, d, c: (0, d)),
                  pl.BlockSpec((1, 1, 2, SSM_STATE, 256), lambda b_, d, c: (b_, d, 0, 0, 0))],
        out_specs=[pl.BlockSpec((1, 1, q_len, 512), lambda b_, d, c: (d, b_, cidx(d, c), 0)),
                   pl.BlockSpec((1, 1, 2, SSM_STATE, 256), lambda b_, d, c: (b_, d, 0, 0, 0))],
        out_shape=[jax.ShapeDtypeStruct((2, b, l, 512), F32),
                   jax.ShapeDtypeStruct((b, 2, 2, SSM_STATE, 256), F32)],
        scratch_shapes=[pltpu.VMEM((2, SSM_STATE, 256), F32)],
        compiler_params=_cp("parallel", "arbitrary", "arbitrary"),
        name="ssd_scan",
    )(u, dt_raw, dtb, a_full, h0)


def _ssd_finish_kernel(yf_ref, yb_ref, xs_ref, z_ref, dsk_ref, g_ref, o_ref):
    y = yf_ref[0, 0] + yb_ref[0, 0] + dsk_ref[...] * xs_ref[0].astype(F32)
    y = y * _silu(z_ref[0].astype(F32))
    ms = jnp.mean(y * y, axis=-1, keepdims=True)
    o_ref[0] = (y * lax.rsqrt(ms + RMS_EPS) * g_ref[...]).astype(BF16)


def _ssd_finish(y, u, p, dsk, g, tm):
    _, b, l, _ = y.shape
    vec = pl.BlockSpec((1, 512), lambda b_, i: (0, 0))
    return pl.pallas_call(
        _ssd_finish_kernel,
        grid=(b, l // tm),
        in_specs=[pl.BlockSpec((1, 1, tm, 512), lambda b_, i: (0, b_, i, 0)),
                  pl.BlockSpec((1, 1, tm, 512), lambda b_, i: (1, b_, i, 0)),
                  pl.BlockSpec((1, tm, 512), lambda b_, i: (b_, i, 0)),
                  pl.BlockSpec((1, tm, 512), lambda b_, i: (b_, i, C512_Z)),
                  vec, vec],
        out_specs=pl.BlockSpec((1, tm, 512), lambda b_, i: (b_, i, 0)),
        out_shape=jax.ShapeDtypeStruct((b, l, 512), BF16),
        compiler_params=_cp("parallel", "parallel"),
        name="ssd_finish",
    )(y, y, u, p, dsk, g)


def _merge_kernel(*refs, alpha, moe):
    if moe:
        (x_ref, mod_ref, oa, ob, oc, od, wg_ref, bg_ref, wb_ref, wo_ref, lng_ref, lnb_ref, wr_ref, br_ref,
         x1_ref, h2_ref, lg_ref) = refs
    else:
        (x_ref, mod_ref, oa, ob, oc, od, wg_ref, bg_ref, wb_ref, wo_ref, lng_ref, lnb_ref, x1_ref) = refs
    m = mod_ref[0]
    h = (x_ref[0] * (1.0 + m[1:2]) + m[0:1]).astype(BF16)
    acc = None
    for k, br in enumerate((oa, ob, oc, od)):
        term = _sigmoid(_dot(h, wg_ref[k]) + bg_ref[k]) * _dot(br[0], wb_ref[k])
        acc = term if acc is None else acc + term
    o = _dot(acc.astype(BF16), wo_ref[...])
    x1 = _layer_norm(alpha * x_ref[0] + m[2:3] * o, lng_ref[...], lnb_ref[...])
    x1_ref[0] = x1
    if moe:
        h2 = x1 * (1.0 + m[4:5]) + m[3:4]
        h2_ref[0] = h2
        h2_hi = h2.astype(BF16)
        h2_lo = (h2 - h2_hi.astype(F32)).astype(BF16)
        lg_ref[0] = (_dot(h2_hi, wr_ref[0]) + _dot(h2_lo, wr_ref[0]) + _dot(h2_hi, wr_ref[1])) + br_ref[...]


def _merge(x3, mod, branches, wg, bg, wb, wo, lng, lnb, alpha, tm, router=None):
    g, r, d = x3.shape
    moe = router is not None
    tile = lambda w: pl.BlockSpec((1, tm, w), lambda b, t: (b, t, 0))
    once = lambda shape: pl.BlockSpec(shape, lambda b, t: (0,) * len(shape), pipeline_mode=pl.Buffered(1))
    in_specs = [tile(d), pl.BlockSpec((1, 8, d), lambda b, t: (b, 0, 0)),
                tile(BRANCH_W), tile(BRANCH_W), tile(BRANCH_W), tile(BRANCH_W),
                once((N_BRANCH, d, d)), once((N_BRANCH, 1, d)), once((N_BRANCH, BRANCH_W, d)), once((d, d)),
                once((1, d)), once((1, d))]
    args = [x3, mod, *branches, wg, bg, wb, wo, lng, lnb]
    out_specs = [tile(d)]
    out_shape = [jax.ShapeDtypeStruct((g, r, d), F32)]
    if moe:
        in_specs += [once((2, d, 128)), once((1, 128))]
        args += list(router)
        out_specs += [tile(d), tile(128)]
        out_shape += [jax.ShapeDtypeStruct((g, r, d), F32), jax.ShapeDtypeStruct((g, r, 128), F32)]
    return pl.pallas_call(
        functools.partial(_merge_kernel, alpha=alpha, moe=moe),
        grid=(g, r // tm),
        in_specs=in_specs, out_specs=out_specs, out_shape=out_shape,
        compiler_params=_cp("parallel", "parallel"),
        name="merge_ln1",
    )(*args)


FFN_CHUNK = D_FF // 2


def _ffn_kernel(x_ref, mod_ref, wg_ref, wu_ref, wd_ref, lng_ref, lnb_ref, o_ref, h_scr, acc_scr, *, alpha):
    j = pl.program_id(2)
    m = mod_ref[0]

    @pl.when(j == 0)
    def _():
        h_scr[...] = (x_ref[0] * (1.0 + m[4:5]) + m[3:4]).astype(BF16)
        acc_scr[...] = jnp.zeros_like(acc_scr)

    h = h_scr[...]
    a = _silu(_dot(h, wg_ref[...])) * _dot(h, wu_ref[...])
    acc_scr[...] += _dot(a.astype(BF16), wd_ref[...])

    @pl.when(j == pl.num_programs(2) - 1)
    def _():
        o_ref[0] = _layer_norm(alpha * x_ref[0] + m[5:6] * acc_scr[...], lng_ref[...], lnb_ref[...])


def _ffn(x3, mod, w_up, w_down, lng, lnb, alpha, tm):
    g, r, d = x3.shape
    n_j = D_FF // FFN_CHUNK
    vec = pl.BlockSpec((1, d), lambda b, t, j: (0, 0))
    return pl.pallas_call(
        functools.partial(_ffn_kernel, alpha=alpha),
        grid=(g, r // tm, n_j),
        in_specs=[pl.BlockSpec((1, tm, d), lambda b, t, j: (b, t, 0)),
                  pl.BlockSpec((1, 8, d), lambda b, t, j: (b, 0, 0)),
                  pl.BlockSpec((d, FFN_CHUNK), lambda b, t, j: (0, j)),
                  pl.BlockSpec((d, FFN_CHUNK), lambda b, t, j: (0, n_j + j)),
                  pl.BlockSpec((FFN_CHUNK, d), lambda b, t, j: (j, 0)), vec, vec],
        out_specs=pl.BlockSpec((1, tm, d), lambda b, t, j: (b, t, 0)),
        out_shape=jax.ShapeDtypeStruct((g, r, d), F32),
        scratch_shapes=[pltpu.VMEM((tm, d), BF16), pltpu.VMEM((tm, d), F32)],
        compiler_params=_cp("parallel", "parallel", "arbitrary"),
        name="ffn_ln2",
    )(x3, mod, w_up, w_up, w_down, lng, lnb)


ROUTE_TM = 256


def _route_kernel(lg_ref, info_ref, cnt_ref, carry):
    i = pl.program_id(0)

    @pl.when(i == 0)
    def _():
        carry[...] = jnp.zeros_like(carry)

    lg = lg_ref[...]
    tm = lg.shape[0]
    lane = lax.broadcasted_iota(jnp.int32, lg.shape, 1).astype(F32)
    m1 = lg.max(axis=-1, keepdims=True)
    i1 = jnp.where(lg == m1, lane, 128.0).min(axis=-1, keepdims=True)
    lg2 = jnp.where(lane == i1, -jnp.inf, lg)
    m2 = lg2.max(axis=-1, keepdims=True)
    i2 = jnp.where(lg2 == m2, lane, 128.0).min(axis=-1, keepdims=True)
    e = jnp.exp(m2 - m1)
    w1 = 1.0 / (1.0 + e)
    w2 = e / (1.0 + e)
    sel = jnp.where((lane == i1) | (lane == i2), 1.0, 0.0).astype(F32)
    row = lax.broadcasted_iota(jnp.int32, (tm, tm), 0)
    col = lax.broadcasted_iota(jnp.int32, (tm, tm), 1)
    before = jnp.where(row > col, 1.0, 0.0).astype(BF16)
    excl = _dot(before, sel.astype(BF16)) + carry[0:1, :]
    r1 = jnp.where(lane == i1, excl, 0.0).sum(axis=-1, keepdims=True)
    r2 = jnp.where(lane == i2, excl, 0.0).sum(axis=-1, keepdims=True)
    carry[...] = carry[...] + sel.sum(axis=0, keepdims=True)
    info = jnp.where(lane == 0, i1, 0.0)
    info = jnp.where(lane == 1, i2, info)
    info = jnp.where(lane == 2, r1, info)
    info = jnp.where(lane == 3, r2, info)
    info = jnp.where(lane == 4, w1, info)
    info = jnp.where(lane == 5, w2, info)
    info_ref[...] = info
    cnt_ref[...] = carry[...]


def _route(logits):
    t = logits.shape[0]
    return pl.pallas_call(
        _route_kernel,
        grid=(t // ROUTE_TM,),
        in_specs=[pl.BlockSpec((ROUTE_TM, 128), lambda i: (i, 0))],
        out_specs=[pl.BlockSpec((ROUTE_TM, 128), lambda i: (i, 0)), pl.BlockSpec((8, 128), lambda i: (0, 0))],
        out_shape=[jax.ShapeDtypeStruct((t, 128), F32), jax.ShapeDtypeStruct((8, 128), F32)],
        scratch_shapes=[pltpu.VMEM((8, 128), F32)],
        compiler_params=_cp("arbitrary"),
        name="moe_route",
    )(logits)


SCATTER_TM = 512


def _scatter_kernel(d_ref, tok_ref, zero_ref, out_ref, sem):
    del zero_ref

    def copy(t, k):
        return pltpu.make_async_copy(tok_ref.at[pl.ds(t, 1)], out_ref.at[pl.ds(d_ref[0, 0, 2 * t + k], 1)], sem)

    def issue(t, carry):
        copy(t, 0).start()
        copy(t, 1).start()
        return carry

    def drain(t, carry):
        copy(t, 0).wait()
        copy(t, 1).wait()
        return carry

    lax.fori_loop(0, SCATTER_TM, issue, 0, unroll=8)
    lax.fori_loop(0, SCATTER_TM, drain, 0, unroll=8)


def _scatter_rows(dest, tok, n_pad):
    t, d = tok.shape
    n_t = t // SCATTER_TM
    return pl.pallas_call(
        _scatter_kernel,
        grid=(n_t,),
        in_specs=[pl.BlockSpec((1, 1, 2 * SCATTER_TM), lambda i: (i, 0, 0), memory_space=pltpu.SMEM),
                  pl.BlockSpec((SCATTER_TM, d), lambda i: (i, 0)),
                  pl.BlockSpec(memory_space=pl.ANY)],
        out_specs=pl.BlockSpec(memory_space=pl.ANY),
        out_shape=jax.ShapeDtypeStruct((n_pad, d), F32),
        scratch_shapes=[pltpu.SemaphoreType.DMA(())],
        input_output_aliases={2: 0},
        compiler_params=_cp("arbitrary"),
        name="moe_scatter",
    )(dest.reshape(n_t, 1, 2 * SCATTER_TM), tok, jnp.zeros((n_pad, d), F32))


EXP_CHUNK = D_FF_EXPERT // 4


def _expert_kernel(be_ref, nv_ref, x_ref, wg_ref, wu_ref, wd_ref, y_ref):
    del be_ref
    b = pl.program_id(0)
    j = pl.program_id(1)

    @pl.when(j == 0)
    def _():
        y_ref[...] = jnp.zeros_like(y_ref)

    @pl.when(b < nv_ref[0])
    def _():
        h = x_ref[...].astype(BF16)
        a = _silu(_dot(h, wg_ref[0])) * _dot(h, wu_ref[0])
        y_ref[...] += _dot(a.astype(BF16), wd_ref[0])


def _experts(x_pad, blk_expert, n_valid, w_up, w_down):
    n_pad, d = x_pad.shape
    n_blk = n_pad // MOE_BLOCK
    n_j = D_FF_EXPERT // EXP_CHUNK

    def jj(b, j, nv):
        return jnp.where(b < nv[0], j, n_j - 1)

    grid_spec = pltpu.PrefetchScalarGridSpec(
        num_scalar_prefetch=2,
        grid=(n_blk, n_j),
        in_specs=[pl.BlockSpec((MOE_BLOCK, d), lambda b, j, be, nv: (b, 0)),
                  pl.BlockSpec((1, d, EXP_CHUNK), lambda b, j, be, nv: (be[b], 0, jj(b, j, nv))),
                  pl.BlockSpec((1, d, EXP_CHUNK), lambda b, j, be, nv: (be[b], 0, n_j + jj(b, j, nv))),
                  pl.BlockSpec((1, EXP_CHUNK, d), lambda b, j, be, nv: (be[b], jj(b, j, nv), 0))],
        out_specs=pl.BlockSpec((MOE_BLOCK, d), lambda b, j, be, nv: (b, 0)),
    )
    return pl.pallas_call(
        _expert_kernel,
        grid_spec=grid_spec,
        out_shape=jax.ShapeDtypeStruct((n_pad, d), F32),
        compiler_params=_cp("arbitrary", "arbitrary"),
        name="moe_experts",
    )(blk_expert, n_valid, x_pad, w_up, w_up, w_down)


COMBINE_TM = 512


def _combine_kernel(d_ref, info_ref, x_ref, mod_ref, lng_ref, lnb_ref, y_ref, o_ref, gbuf, sem, *, alpha):
    def copy(t, k):
        return pltpu.make_async_copy(y_ref.at[pl.ds(d_ref[0, 0, 2 * t + k], 1)], gbuf.at[k, pl.ds(t, 1)], sem)

    def issue(t, carry):
        copy(t, 0).start()
        copy(t, 1).start()
        return carry

    def drain(t, carry):
        copy(t, 0).wait()
        copy(t, 1).wait()
        return carry

    lax.fori_loop(0, COMBINE_TM, issue, 0, unroll=8)
    lax.fori_loop(0, COMBINE_TM, drain, 0, unroll=8)
    m = mod_ref[0]
    info = info_ref[0]
    f = info[:, 4:5] * gbuf[0] + info[:, 5:6] * gbuf[1]
    o_ref[0] = _layer_norm(alpha * x_ref[0] + m[5:6] * f, lng_ref[...], lnb_ref[...])


def _combine(dest, info, x3, mod, lng, lnb, y_pad, alpha):
    g, r, d = x3.shape
    n_t = r // COMBINE_TM
    vec = pl.BlockSpec((1, d), lambda b, i: (0, 0))
    return pl.pallas_call(
        functools.partial(_combine_kernel, alpha=alpha),
        grid=(g, n_t),
        in_specs=[pl.BlockSpec((1, 1, 2 * COMBINE_TM), lambda b, i: (b * n_t + i, 0, 0), memory_space=pltpu.SMEM),
                  pl.BlockSpec((1, COMBINE_TM, 128), lambda b, i: (b, i, 0)),
                  pl.BlockSpec((1, COMBINE_TM, d), lambda b, i: (b, i, 0)),
                  pl.BlockSpec((1, 8, d), lambda b, i: (b, 0, 0)), vec, vec,
                  pl.BlockSpec(memory_space=pl.ANY)],
        out_specs=pl.BlockSpec((1, COMBINE_TM, d), lambda b, i: (b, i, 0)),
        out_shape=jax.ShapeDtypeStruct((g, r, d), F32),
        scratch_shapes=[pltpu.VMEM((2, COMBINE_TM, d), F32), pltpu.SemaphoreType.DMA(())],
        compiler_params=_cp("arbitrary", "arbitrary"),
        name="moe_combine_ln2",
    )(dest.reshape(g * n_t, 1, 2 * COMBINE_TM), info.reshape(g, r, 128), x3, mod, lng, lnb, y_pad)


def _moe_plan(info, counts_row):
    n_tok = info.shape[0]
    e = info[:, 0:2].astype(jnp.int32)
    rank = info[:, 2:4].astype(jnp.int32)
    counts = counts_row[:N_EXPERTS].astype(jnp.int32)
    padded = (counts + MOE_BLOCK - 1) // MOE_BLOCK * MOE_BLOCK
    pad_end = jnp.cumsum(padded)
    pad_start = pad_end - padded
    dest = pad_start[e] + rank
    n_blk = -(-n_tok * 2 // MOE_BLOCK) + N_EXPERTS
    blk_start = jnp.arange(n_blk, dtype=jnp.int32) * MOE_BLOCK
    blk_expert = jnp.minimum(jnp.sum(pad_end[None, :] <= blk_start[:, None], axis=1), N_EXPERTS - 1).astype(jnp.int32)
    n_valid = (pad_end[-1:] // MOE_BLOCK).astype(jnp.int32)
    return dest, blk_expert, n_valid, n_blk * MOE_BLOCK


def _pad_rows(a, rows):
    return jnp.concatenate([a, jnp.zeros((rows - a.shape[0],) + a.shape[1:], a.dtype)], axis=0)


def kernel(x, c, ctx, c_ctx, w_mod, b_mod, w_in, w_gate, b_gate, na_rpb, swa_sink, qk_gain_q, qk_gain_k, conv_w, conv_b, dt_bias, a_log, d_skip, ssm_norm_g, w_branch, w_out, ln1_g, ln1_b, ln2_g, ln2_b, ffn_w_up, ffn_w_down, moe_w_router, moe_b_router, moe_w_up, moe_w_down):
    bsz, seq, d = x.shape
    lc = ctx.shape[1]
    depth = w_mod.shape[0]
    alpha = (2 * depth) ** 0.25
    n_ctx = bsz * lc
    assert bsz < 8 and seq % 1024 == 0 and lc % 256 == 0 and d == D_MODEL
    assert depth < 3 or n_ctx % max(SCATTER_TM, COMBINE_TM) == 0

    s_rows = _pad_rows(jnp.concatenate([c, c_ctx[None, :]], axis=0), 8)
    mods = _modulation(s_rows, w_mod, b_mod).reshape(depth, 8, 6, d)
    mods = jnp.concatenate([mods, jnp.zeros((depth, 8, 2, d), F32)], axis=2)

    offs = [0, 512, 1024, 1536, 2048, 2176, 2304, 2816, 2944, 3072, 3584, 4608, 4624]
    seg = lambda k: w_in[:, :, offs[k]:offs[k + 1]]
    w_main = jnp.concatenate([seg(0), seg(1), seg(2), seg(3), seg(6), seg(9), seg(10), seg(4), seg(5), seg(7), seg(8)],
                             axis=-1).astype(BF16)
    zpad = jnp.zeros((depth, d, 128 - SSM_HEADS), F32)
    w_dt = jnp.concatenate([seg(11)[:, :, :SSM_HEADS], zpad, seg(11)[:, :, SSM_HEADS:], zpad], axis=-1).astype(BF16)

    w_gate_b = w_gate.astype(BF16)
    w_branch_b = w_branch.astype(BF16)
    w_out_b = w_out.astype(BF16)
    ffn_up_b = ffn_w_up.astype(BF16)
    ffn_down_b = ffn_w_down.astype(BF16)
    moe_up_b = moe_w_up.astype(BF16)
    moe_down_b = moe_w_down.astype(BF16)

    rope_tabs = _rope_tables(seq)
    a_neg = -jnp.exp(a_log.astype(F32))
    pad8 = lambda v: jnp.concatenate([v, jnp.zeros(v.shape[:-1] + (128 - SSM_HEADS,), F32)], axis=-1)

    ctx2 = ctx.reshape(1, n_ctx, d)
    for l in range(depth):
        with_ctx = l < depth - 1
        mod_lat = mods[l, :bsz]
        mod_ctx = mods[l, bsz:bsz + 1]

        p, dt_raw = _inproj(x, mod_lat, w_main[l], w_dt[l], 1024)
        pc, dtc_raw = _inproj(ctx2, mod_ctx, w_main[l], w_dt[l], n_ctx)
        pc = pc.reshape(bsz, lc, P_COLS)
        dtc_raw = dtc_raw.reshape(bsz, lc, DT_COLS)

        gq = jnp.tile(qk_gain_q[l].astype(F32), 2)[None, :]
        gk = jnp.tile(qk_gain_k[l].astype(F32), 2)[None, :]
        swq, gaq_pad, swk, gak, gavt = _prep(p, rope_tabs, gq, gk, 1024, True, True, GA_TK)
        swq_c, gaq_c, swk_c, gak_c, gavt_c = _prep(pc, rope_tabs, gq, gk, lc, False, False, lc)

        sink_tab = jnp.broadcast_to(swa_sink[l].astype(F32)[:, None], (8, 128))
        o_a = _na_attention(p, pc, _na_bias_table(na_rpb[l]))
        o_b = _swa_attention(swq, swk, p, swk_c, pc, sink_tab)
        o_c = _ga_attention(gaq_pad, gak, gavt, gak_c, gavt_c)

        w8 = _pad_rows(conv_w[l].astype(F32), 8)
        cb = conv_b[l].astype(F32)[None, :]
        u = _conv(p, w8, cb, 512)
        uc = _conv(pc, w8, cb, lc)
        dtb = jnp.concatenate([pad8(dt_bias[l, 0].astype(F32)), pad8(dt_bias[l, 1].astype(F32))])[None, :]
        a_full = jnp.concatenate([pad8(a_neg[l, 0]), pad8(a_neg[l, 1])])[None, :]
        h_zero = jnp.zeros((bsz, 2, 2, SSM_STATE, 256), F32)
        yc, h_ctx = _ssd(uc, dtc_raw, dtb, a_full, h_zero, lc)
        y, _ = _ssd(u, dt_raw, dtb, a_full, h_ctx, 256)
        dsk = jnp.repeat(d_skip[l].astype(F32), HEAD_DIM)[None, :]
        ng = ssm_norm_g[l].astype(F32)[None, :]
        o_d = _ssd_finish(y, u, p, dsk, ng, 1024)

        is_moe = l % 2 == 1
        i = l // 2
        router = None
        if is_moe:
            wr = jnp.concatenate([moe_w_router[i].astype(F32), jnp.zeros((d, 128 - N_EXPERTS), F32)], axis=-1)
            br = jnp.concatenate([moe_b_router[i].astype(F32), jnp.full((128 - N_EXPERTS,), NEG_INF, F32)])[None, :]
            wr_hi = wr.astype(BF16)
            wr_lo = (wr - wr_hi.astype(F32)).astype(BF16)
            router = (jnp.stack([wr_hi, wr_lo]), br)
        lng1, lnb1 = ln1_g[l].astype(F32)[None, :], ln1_b[l].astype(F32)[None, :]
        lng2, lnb2 = ln2_g[l].astype(F32)[None, :], ln2_b[l].astype(F32)[None, :]
        bg = b_gate[l].astype(F32)[:, None, :]
        res = _merge(x, mod_lat, (o_a, o_b, o_c, o_d), w_gate_b[l], bg, w_branch_b[l], w_out_b[l], lng1, lnb1,
                     alpha, 512, router)
        if with_ctx:
            o_ac = _ctx_attention(pc, C512_NAQ, pc, C512_NAK, pc, C512_NAV, 8, sink_tab, True, False)
            o_bc = _ctx_attention(swq_c, 0, swk_c, 0, pc, C128_SWV, 2, sink_tab, False, True)
            o_cc = _ctx_attention(gaq_c, 0, gak_c, 0, pc, C128_GAV, 2, sink_tab, False, False)
            o_dc = _ssd_finish(yc, uc, pc, dsk, ng, lc)
            to2 = lambda t: t.reshape(1, n_ctx, BRANCH_W)
            res_c = _merge(ctx2, mod_ctx, (to2(o_ac), to2(o_bc), to2(o_cc), to2(o_dc)), w_gate_b[l], bg,
                           w_branch_b[l], w_out_b[l], lng1, lnb1, alpha, min(512, n_ctx), router)

        if not is_moe:
            x = _ffn(res[0], mod_lat, ffn_up_b[i], ffn_down_b[i], lng2, lnb2, alpha, 1024)
            if with_ctx:
                ctx2 = _ffn(res_c[0], mod_ctx, ffn_up_b[i], ffn_down_b[i], lng2, lnb2, alpha, n_ctx)
        else:
            x1, h2, lg = res
            tok = h2.reshape(bsz * seq, d)
            lgs = lg.reshape(bsz * seq, 128)
            if with_ctx:
                x1c, h2c, lgc = res_c
                tok = jnp.concatenate([tok, h2c.reshape(n_ctx, d)], axis=0)
                lgs = jnp.concatenate([lgs, lgc.reshape(n_ctx, 128)], axis=0)
            info, cnt = _route(lgs)
            dest, blk_expert, n_valid, n_pad = _moe_plan(info, cnt[0])
            x_pad = _scatter_rows(dest, tok, n_pad)
            y_pad = _experts(x_pad, blk_expert, n_valid, moe_up_b[i], moe_down_b[i])
            n_lat = bsz * seq
            x = _combine(dest[:n_lat], info[:n_lat], x1, mod_lat, lng2, lnb2, y_pad, alpha)
            if with_ctx:
                ctx2 = _combine(dest[n_lat:], info[n_lat:], x1c, mod_ctx, lng2, lnb2, y_pad, alpha)
    return x
```
